```python
import math
import jax, jax.numpy as jnp
from jax import lax
import numpy as np

D_MODEL = 1024
BATCH = 2
SEQ = 16384
DEPTH = 2

LRU_WIDTH = D_MODEL
LRU_BLOCKS = 8
LRU_BLOCK = LRU_WIDTH // LRU_BLOCKS
CONV_WIDTH = 4
LRU_C = 8.0
POOL_WIDTH = D_MODEL
POOL_WINDOWS = (2, 4, 8, 16)
POOL_GROUPS = len(POOL_WINDOWS)
POOL_GROUP = POOL_WIDTH // POOL_GROUPS
ATT_HEADS = 8
ATT_HEAD_DIM = 64
ATT_V_DIM = 2 * ATT_HEAD_DIM
QK_WIDTH = ATT_HEADS * 2 * ATT_HEAD_DIM
ATT_WIDTH = ATT_HEADS * ATT_V_DIM
ROPE_THETA = 10000.0
Q_BLOCK = 128
N_BRANCH = 3
BRANCH_WIDTH = D_MODEL
IN_SIZES = (LRU_WIDTH, LRU_WIDTH, POOL_WIDTH, QK_WIDTH, QK_WIDTH, ATT_WIDTH, N_BRANCH * D_MODEL)
IN_WIDTH = sum(IN_SIZES)
N_EXPERTS = 16
N_GROUPS = 4
EXPERTS_PER_GROUP = N_EXPERTS // N_GROUPS
TOP_K = 2
D_EXPERT = 512
MOE_CHUNK = 2048
EPS = 1e-6

kernel_name = "hybrid_lru_pool_diffattn_groupmoe_adaln"


def rms_norm(x, g):
    xf = x.astype(jnp.float32)
    y = xf * lax.rsqrt(jnp.mean(xf * xf, axis=-1, keepdims=True) + EPS)
    return (y * g.astype(jnp.float32)).astype(x.dtype)


def split_points():
    pts, acc = [], 0
    for s in IN_SIZES[:-1]:
        acc += s
        pts.append(acc)
    return pts


def rope_tables(seq):
    pos = jnp.arange(seq, dtype=jnp.float32)
    inv = ROPE_THETA ** (-jnp.arange(0, ATT_HEAD_DIM, 2, dtype=jnp.float32) / ATT_HEAD_DIM)
    ang = pos[:, None] * inv[None, :]
    return jnp.cos(ang), jnp.sin(ang)


def apply_rope(x, cos, sin):
    x1, x2 = jnp.split(x, 2, axis=-1)
    c = cos[None, :, None, None, :]
    s = sin[None, :, None, None, :]
    return jnp.concatenate([x1 * c - x2 * s, x2 * c + x1 * s], axis=-1).astype(x.dtype)


def rg_lru_branch(xb, yb, conv_w, conv_b, wa, ba, wx, bx, lru_l):
    B, S, W = xb.shape
    xp = jnp.pad(xb, ((0, 0), (CONV_WIDTH - 1, 0), (0, 0)))
    xc = conv_b
    for j in range(CONV_WIDTH):
        xc = xc + xp[:, j:j + S] * conv_w[j]
    xr = xc.reshape(B, S, LRU_BLOCKS, LRU_BLOCK)
    gate_a = jax.nn.sigmoid(jnp.einsum('bsnc,ncd->bsnd', xr, wa) + ba).reshape(B, S, W)
    gate_x = jax.nn.sigmoid(jnp.einsum('bsnc,ncd->bsnd', xr, wx) + bx).reshape(B, S, W)
    log_a = LRU_C * gate_a.astype(jnp.float32) * jax.nn.log_sigmoid(lru_l.astype(jnp.float32))
    a = jnp.exp(log_a)
    u = jnp.sqrt(-jnp.expm1(2.0 * log_a)) * (gate_x * xc).astype(jnp.float32)

    def combine(left, right):
        a1, b1 = left
        a2, b2 = right
        return a1 * a2, a2 * b1 + b2

    _, h = lax.associative_scan(combine, (a, u), axis=1)
    return h.astype(xb.dtype) * jax.nn.gelu(yb)


def pool_branch(xp, pool_w, pool_scale):
    B, S, W = xp.shape
    xg = xp.reshape(B, S, POOL_GROUPS, POOL_GROUP).astype(jnp.float32)
    cs = jnp.pad(jnp.cumsum(xg, axis=1), ((0, 0), (1, 0), (0, 0), (0, 0)))
    t = jnp.arange(S)
    outs = []
    for g, w in enumerate(POOL_WINDOWS):
        win_sum = cs[:, 1:, g] - jnp.pad(cs[:, :S + 1 - w, g], ((0, 0), (w - 1, 0), (0, 0)))
        cnt = jnp.minimum(t + 1, w).astype(jnp.float32)[None, :, None]
        outs.append(win_sum / cnt - xg[:, :, g])
    pooled = jnp.stack(outs, axis=2).astype(xp.dtype)
    y = jnp.einsum('bsgc,gcd->bsgd', pooled, pool_w).reshape(B, S, W)
    return y * pool_scale


def diff_attention(q, k, v, qn_g, kn_g, lq1, lk1, lq2, lk2, subln_g, lam_init, cos, sin):
    B, S, _ = q.shape
    q = q.reshape(B, S, ATT_HEADS, 2, ATT_HEAD_DIM)
    k = k.reshape(B, S, ATT_HEADS, 2, ATT_HEAD_DIM)
    v = v.reshape(B, S, ATT_HEADS, ATT_V_DIM)
    q = apply_rope(rms_norm(q, qn_g), cos, sin)
    k = apply_rope(rms_norm(k, kn_g), cos, sin)
    lam = (jnp.exp(jnp.sum(lq1.astype(jnp.float32) * lk1.astype(jnp.float32)))
           - jnp.exp(jnp.sum(lq2.astype(jnp.float32) * lk2.astype(jnp.float32))) + lam_init)
    n_blk = S // Q_BLOCK
    qb = q.reshape(B, n_blk, Q_BLOCK, ATT_HEADS, 2, ATT_HEAD_DIM).transpose(1, 0, 3, 4, 2, 5)
    kt = k.transpose(0, 2, 3, 1, 4)
    vt = v.transpose(0, 2, 1, 3)
    starts = jnp.arange(n_blk, dtype=jnp.int32) * Q_BLOCK
    kpos = jnp.arange(S, dtype=jnp.int32)
    scale = ATT_HEAD_DIM ** -0.5

    def block(args):
        q_blk, start = args
        s = jnp.einsum('bhcqd,bhckd->bhcqk', q_blk, kt, preferred_element_type=jnp.float32) * scale
        qpos = start + jnp.arange(Q_BLOCK, dtype=jnp.int32)
        mask = kpos[None, :] <= qpos[:, None]
        p = jax.nn.softmax(jnp.where(mask, s, -jnp.inf), axis=-1)
        attn = p[:, :, 0] - lam * p[:, :, 1]
        return jnp.einsum('bhqk,bhkv->bhqv', attn.astype(vt.dtype), vt)

    o = lax.map(block, (qb, starts))
    o = o.transpose(1, 0, 3, 2, 4).reshape(B, S, ATT_HEADS, ATT_V_DIM)
    o = rms_norm(o, subln_g) * (1.0 - lam_init)
    return o.reshape(B, S, ATT_WIDTH)


def group_moe(h, router_w, router_b, w_gate, w_up, w_down):
    B, S, D = h.shape
    hf = h.reshape(-1, D)
    N = hf.shape[0]
    aff = jax.nn.sigmoid(jnp.matmul(hf, router_w, preferred_element_type=jnp.float32))
    sel = (aff + router_b.astype(jnp.float32)).reshape(N, N_GROUPS, EXPERTS_PER_GROUP)
    gscore = lax.top_k(sel, TOP_K)[0].sum(-1)
    gidx = jnp.argmax(gscore, axis=-1)
    in_group = sel[jnp.arange(N), gidx]
    _, local = lax.top_k(in_group, TOP_K)
    expert = gidx[:, None] * EXPERTS_PER_GROUP + local
    wts = jnp.take_along_axis(aff, expert, axis=1)
    wts = wts / jnp.sum(wts, axis=-1, keepdims=True)
    comb = jnp.zeros((N, N_EXPERTS), jnp.float32).at[jnp.arange(N)[:, None], expert].set(wts)
    pad = (-N) % MOE_CHUNK
    hp = jnp.pad(hf, ((0, pad), (0, 0)))
    cp = jnp.pad(comb, ((0, pad), (0, 0)))
    n_chunk = (N + pad) // MOE_CHUNK

    def chunk(args):
        xt, ct = args
        g = jnp.einsum('td,edf->tef', xt, w_gate)
        u = jnp.einsum('td,edf->tef', xt, w_up)
        act = jax.nn.silu(g) * u * ct[:, :, None].astype(xt.dtype)
        return jnp.einsum('tef,efd->td', act, w_down)

    y = lax.map(chunk, (hp.reshape(n_chunk, MOE_CHUNK, D), cp.reshape(n_chunk, MOE_CHUNK, N_EXPERTS)))
    return y.reshape(-1, D)[:N].reshape(B, S, D)


def setup_inputs(seed: int = 0) -> dict:
    key = jax.random.key(seed)
    ks = jax.random.split(key, 32)
    L, D, E, F = DEPTH, D_MODEL, N_EXPERTS, D_EXPERT

    def nrm(k, shape, scale):
        return jax.random.normal(k, shape, jnp.float32) * scale

    def gain(k, shape):
        return 1.0 + 0.02 * jax.random.normal(k, shape, jnp.float32)

    u = jax.random.uniform(ks[10], (L, LRU_WIDTH), jnp.float32, 0.9, 0.999)
    s = u ** (1.0 / LRU_C)
    lru_l = jnp.log(s) - jnp.log1p(-s)
    return {
        "x": nrm(ks[0], (BATCH, SEQ, D), 1.0),
        "c": nrm(ks[1], (BATCH, D), 1.0),
        "w_ada": nrm(ks[2], (L, D, 6 * D), 0.5 * D ** -0.5),
        "b_ada": nrm(ks[3], (L, 6 * D), 0.01),
        "norm1_g": gain(ks[4], (L, D)),
        "norm2_g": gain(ks[5], (L, D)),
        "w_in": nrm(ks[6], (L, D, IN_WIDTH), D ** -0.5),
        "conv_w": nrm(ks[7], (L, CONV_WIDTH, LRU_WIDTH), CONV_WIDTH ** -0.5),
        "conv_b": nrm(ks[8], (L, LRU_WIDTH), 0.01),
        "lru_wa": nrm(ks[9], (L, LRU_BLOCKS, LRU_BLOCK, LRU_BLOCK), LRU_BLOCK ** -0.5),
        "lru_ba": nrm(ks[11], (L, LRU_BLOCKS, LRU_BLOCK), 0.01),
        "lru_wx": nrm(ks[12], (L, LRU_BLOCKS, LRU_BLOCK, LRU_BLOCK), LRU_BLOCK ** -0.5),
        "lru_bx": nrm(ks[13], (L, LRU_BLOCKS, LRU_BLOCK), 0.01),
        "lru_l": lru_l,
        "pool_w": nrm(ks[14], (L, POOL_GROUPS, POOL_GROUP, POOL_GROUP), POOL_GROUP ** -0.5),
        "pool_scale": gain(ks[15], (L, POOL_WIDTH)),
        "qn_g": gain(ks[16], (L, ATT_HEAD_DIM)),
        "kn_g": gain(ks[17], (L, ATT_HEAD_DIM)),
        "lam_q1": nrm(ks[18], (L, ATT_HEAD_DIM), 0.1),
        "lam_k1": nrm(ks[19], (L, ATT_HEAD_DIM), 0.1),
        "lam_q2": nrm(ks[20], (L, ATT_HEAD_DIM), 0.1),
        "lam_k2": nrm(ks[21], (L, ATT_HEAD_DIM), 0.1),
        "subln_g": gain(ks[22], (L, ATT_V_DIM)),
        "w_branch": nrm(ks[23], (L, N_BRANCH, BRANCH_WIDTH, D), BRANCH_WIDTH ** -0.5),
        "w_out": nrm(ks[24], (L, D, D), D ** -0.5),
        "router_w": nrm(ks[25], (D, E), D ** -0.5),
        "router_b": nrm(ks[26], (E,), 0.01),
        "w_gate": nrm(ks[27], (L, E, D, F), D ** -0.5),
        "w_up": nrm(ks[28], (L, E, D, F), D ** -0.5),
        "w_down": nrm(ks[29], (L, E, F, D), F ** -0.5),
    }


def reference(x, c, w_ada, b_ada, norm1_g, norm2_g, w_in, conv_w, conv_b, lru_wa, lru_ba,
              lru_wx, lru_bx, lru_l, pool_w, pool_scale, qn_g, kn_g, lam_q1, lam_k1, lam_q2,
              lam_k2, subln_g, w_branch, w_out, router_w, router_b, w_gate, w_up, w_down):
    B, S, D = x.shape
    cos, sin = rope_tables(S)
    pts = split_points()
    c_act = jax.nn.silu(c)
    for l in range(DEPTH):
        lam_init = 0.8 - 0.6 * math.exp(-0.3 * l)
        mod = jnp.matmul(c_act, w_ada[l]) + b_ada[l]
        sh1, sc1, g1, sh2, sc2, g2 = jnp.split(mod[:, None, :], 6, axis=-1)
        h = rms_norm(x, norm1_g[l]) * (1.0 + sc1) + sh1
        proj = jnp.matmul(h, w_in[l])
        x_lru, y_lru, x_pool, q, k, v, gates = jnp.split(proj, pts, axis=-1)
        br_lru = rg_lru_branch(x_lru, y_lru, conv_w[l], conv_b[l], lru_wa[l], lru_ba[l],
                               lru_wx[l], lru_bx[l], lru_l[l])
        br_pool = pool_branch(x_pool, pool_w[l], pool_scale[l])
        br_att = diff_attention(q, k, v, qn_g[l], kn_g[l], lam_q1[l], lam_k1[l], lam_q2[l],
                                lam_k2[l], subln_g[l], lam_init, cos, sin)
        gates = jax.nn.sigmoid(gates).reshape(B, S, N_BRANCH, D)
        merged = (gates[:, :, 0] * jnp.matmul(br_lru, w_branch[l, 0])
                  + gates[:, :, 1] * jnp.matmul(br_pool, w_branch[l, 1])
                  + gates[:, :, 2] * jnp.matmul(br_att, w_branch[l, 2]))
        x = x + g1 * jnp.matmul(merged, w_out[l])
        h2 = rms_norm(x, norm2_g[l]) * (1.0 + sc2) + sh2
        x = x + g2 * group_moe(h2, router_w, router_b, w_gate[l], w_up[l], w_down[l])
    return x
```

```python
import functools
import math

import jax
import jax.numpy as jnp
from jax import lax
from jax.experimental import pallas as pl
from jax.experimental.pallas import tpu as pltpu

F32 = jnp.float32
BF16 = jnp.bfloat16

EPS = 1e-6
LANES = 128
SUBLANES = 8
LRU_BLOCKS = 8
CONV_WIDTH = 4
LRU_C = 8.0
POOL_WINDOWS = (2, 4, 8, 16)
POOL_HALO = 16
ATT_HEADS = 8
ATT_HEAD_DIM = 64
ROPE_THETA = 10000.0
N_BRANCH = 3
N_EXPERTS = 16
N_GROUPS = 4
EXPERTS_PER_GROUP = N_EXPERTS // N_GROUPS
VMEM_LIMIT = 56 * 1024 * 1024


def _cparams(sem):
    return pltpu.CompilerParams(dimension_semantics=sem, vmem_limit_bytes=VMEM_LIMIT)


def _dot(a, b):
    return jnp.dot(a, b, preferred_element_type=F32)


def _dot_nt(a, b):
    return lax.dot_general(a, b, (((1,), (1,)), ((), ())), preferred_element_type=F32)


def _mod_kernel(c_ref, w_ref, b_ref, o_ref):
    c = c_ref[...]
    ca = c * jax.nn.sigmoid(c)
    o_ref[0] = jnp.dot(ca, w_ref[0], preferred_element_type=F32,
                       precision=lax.Precision.HIGHEST) + b_ref[0]


def _mod_call(c, w_ada, b_ada):
    L, D, D6 = w_ada.shape
    B = c.shape[0]
    cp = jnp.zeros((SUBLANES, D), F32).at[:B].set(c)
    out = pl.pallas_call(
        _mod_kernel,
        out_shape=jax.ShapeDtypeStruct((L, SUBLANES, D6), F32),
        grid=(L, D6 // D),
        in_specs=[pl.BlockSpec((SUBLANES, D), lambda l, j: (0, 0)),
                  pl.BlockSpec((1, D, D), lambda l, j: (l, 0, j)),
                  pl.BlockSpec((1, 1, D), lambda l, j: (l, 0, j))],
        out_specs=pl.BlockSpec((1, SUBLANES, D), lambda l, j: (l, 0, j)),
        compiler_params=_cparams(("arbitrary", "arbitrary")),
        name="mod",
    )(cp, w_ada, b_ada.reshape(L, 1, D6))
    mod = out[:, :B].reshape(L, B, 6, D)
    return jnp.pad(mod, ((0, 0), (0, 0), (0, 2), (0, 0)))


def _inproj_kernel(x_ref, mod_ref, g_ref, w_ref, o_ref):
    x = x_ref[...]
    m = mod_ref[0]
    var = jnp.mean(x * x, axis=-1, keepdims=True)
    y = x * lax.rsqrt(var + EPS) * g_ref[...]
    h = y * (1.0 + m[1:2]) + m[0:1]
    o_ref[...] = _dot(h.astype(BF16), w_ref[...]).astype(BF16)


def _inproj_call(x, mod, g, w, S, tm, tn):
    N, D = x.shape
    W = w.shape[1]
    tpb = S // tm
    return pl.pallas_call(
        _inproj_kernel,
        out_shape=jax.ShapeDtypeStruct((N, W), BF16),
        grid=(W // tn, N // tm),
        in_specs=[pl.BlockSpec((tm, D), lambda j, i: (i, 0)),
                  pl.BlockSpec((1, SUBLANES, D), lambda j, i: (i // tpb, 0, 0)),
                  pl.BlockSpec((1, D), lambda j, i: (0, 0)),
                  pl.BlockSpec((D, tn), lambda j, i: (0, j))],
        out_specs=pl.BlockSpec((tm, tn), lambda j, i: (i, j)),
        compiler_params=_cparams(("arbitrary", "arbitrary")),
        name="inproj",
    )(x, mod, g, w)


def _lru_kernel(x_ref, y_ref, cw_ref, cb_ref, wa_ref, ba_ref, wx_ref, bx_ref, ll_ref, o_ref,
                xbuf, hcar, *, tpb):
    T, W = x_ref.shape
    blk = W // LRU_BLOCKS
    i = pl.program_id(0)

    @pl.when(i % tpb == 0)
    def _():
        xbuf[0:SUBLANES, :] = jnp.zeros((SUBLANES, W), F32)
        hcar[...] = jnp.zeros_like(hcar)

    xbuf[SUBLANES:SUBLANES + T, :] = x_ref[...].astype(F32)
    xc = cb_ref[...] + xbuf[SUBLANES:SUBLANES + T, :] * cw_ref[CONV_WIDTH - 1:CONV_WIDTH, :]
    for j in range(CONV_WIDTH - 1):
        d = CONV_WIDTH - 1 - j
        xc = xc + xbuf[SUBLANES - d:SUBLANES - d + T, :] * cw_ref[j:j + 1, :]
    xbuf[0:SUBLANES, :] = xbuf[T:T + SUBLANES, :]

    xcb = xc.astype(BF16)
    ga, gx = [], []
    for n in range(LRU_BLOCKS):
        xs = xcb[:, n * blk:(n + 1) * blk]
        ga.append(_dot(xs, wa_ref[n]))
        gx.append(_dot(xs, wx_ref[n]))
    gate_a = jax.nn.sigmoid(jnp.concatenate(ga, axis=1) + ba_ref[...])
    gate_x = jax.nn.sigmoid(jnp.concatenate(gx, axis=1) + bx_ref[...])
    ll = ll_ref[...]
    log_sig = jnp.minimum(ll, 0.0) - jnp.log1p(jnp.exp(-jnp.abs(ll)))
    log_a = LRU_C * gate_a * log_sig
    a = jnp.exp(log_a)
    u = jnp.sqrt(1.0 - a * a) * (gate_x * xc)

    row = lax.broadcasted_iota(jnp.int32, (T, W), 0)
    A, H = a, u
    sh = 1
    while sh < T:
        keep = row >= sh
        A_s = jnp.where(keep, pltpu.roll(A, sh, axis=0), 1.0)
        H_s = jnp.where(keep, pltpu.roll(H, sh, axis=0), 0.0)
        H = A * H_s + H
        A = A * A_s
        sh *= 2
    h = H + A * hcar[...]
    hcar[...] = h[T - 1:T, :]
    o_ref[...] = (h * jax.nn.gelu(y_ref[...].astype(F32))).astype(o_ref.dtype)


def _lru_call(proj, cw, cb, wa, ba, wx, bx, ll, S, W, tm):
    N = proj.shape[0]
    tpb = S // tm
    blk = W // LRU_BLOCKS
    vec = lambda: pl.BlockSpec((1, W), lambda i: (0, 0))
    return pl.pallas_call(
        functools.partial(_lru_kernel, tpb=tpb),
        out_shape=jax.ShapeDtypeStruct((N, W), BF16),
        grid=(N // tm,),
        in_specs=[pl.BlockSpec((tm, W), lambda i: (i, 0)),
                  pl.BlockSpec((tm, W), lambda i: (i, 1)),
                  pl.BlockSpec((CONV_WIDTH, W), lambda i: (0, 0)),
                  vec(),
                  pl.BlockSpec((LRU_BLOCKS, blk, blk), lambda i: (0, 0, 0)),
                  vec(),
                  pl.BlockSpec((LRU_BLOCKS, blk, blk), lambda i: (0, 0, 0)),
                  vec(),
                  vec()],
        out_specs=pl.BlockSpec((tm, W), lambda i: (i, 0)),
        scratch_shapes=[pltpu.VMEM((tm + SUBLANES, W), F32), pltpu.VMEM((1, W), F32)],
        compiler_params=_cparams(("arbitrary",)),
        name="lru",
    )(proj, proj, cw, cb, wa, ba, wx, bx, ll)


def _pool_kernel(x_ref, pw_ref, ps_ref, o_ref, xbuf, *, tpb):
    T, W = x_ref.shape
    G = len(POOL_WINDOWS)
    gw = W // G
    i = pl.program_id(0)

    @pl.when(i % tpb == 0)
    def _():
        xbuf[0:POOL_HALO, :] = jnp.zeros((POOL_HALO, W), F32)

    xbuf[POOL_HALO:POOL_HALO + T, :] = x_ref[...].astype(F32)
    t = (i % tpb) * T + lax.broadcasted_iota(jnp.int32, (T, 1), 0)
    outs = []
    for g, w in enumerate(POOL_WINDOWS):
        cols = slice(g * gw, (g + 1) * gw)
        xg = xbuf[POOL_HALO:POOL_HALO + T, cols]
        ws = xg
        for d in range(1, w):
            ws = ws + xbuf[POOL_HALO - d:POOL_HALO - d + T, cols]
        cnt = jnp.minimum(t + 1, w).astype(F32)
        pooled = (ws / cnt - xg).astype(BF16)
        outs.append(_dot(pooled, pw_ref[g]))
    xbuf[0:POOL_HALO, :] = xbuf[T:T + POOL_HALO, :]
    o_ref[...] = (jnp.concatenate(outs, axis=1) * ps_ref[...]).astype(o_ref.dtype)


def _pool_call(proj, pw, ps, S, W, tm):
    N = proj.shape[0]
    tpb = S // tm
    G = len(POOL_WINDOWS)
    return pl.pallas_call(
        functools.partial(_pool_kernel, tpb=tpb),
        out_shape=jax.ShapeDtypeStruct((N, W), BF16),
        grid=(N // tm,),
        in_specs=[pl.BlockSpec((tm, W), lambda i: (i, 2)),
                  pl.BlockSpec((G, W // G, W // G), lambda i: (0, 0, 0)),
                  pl.BlockSpec((1, W), lambda i: (0, 0))],
        out_specs=pl.BlockSpec((tm, W), lambda i: (i, 0)),
        scratch_shapes=[pltpu.VMEM((tm + POOL_HALO, W), F32)],
        compiler_params=_cparams(("arbitrary",)),
        name="pool",
    )(proj, pw, ps)


def _qkprep_kernel(q_ref, k_ref, cs_ref, sn_ref, qg_ref, kg_ref, qo_ref, ko_ref):
    T, W = q_ref.shape
    hd = ATT_HEAD_DIM
    lane = lax.broadcasted_iota(jnp.int32, (T, LANES), 1)
    first = lane < hd
    lower = (lane % hd) < (hd // 2)
    cs = cs_ref[...]
    sn = sn_ref[...]

    def prep(x_ref, g_ref, o_ref, scale):
        g = g_ref[...]
        for h in range(W // LANES):
            x = x_ref[:, h * LANES:(h + 1) * LANES].astype(F32)
            sq = x * x
            s1 = jnp.sum(jnp.where(first, sq, 0.0), axis=-1, keepdims=True)
            s2 = jnp.sum(jnp.where(first, 0.0, sq), axis=-1, keepdims=True)
            r = jnp.where(first, lax.rsqrt(s1 * (1.0 / hd) + EPS), lax.rsqrt(s2 * (1.0 / hd) + EPS))
            xn = x * r * g
            partner = jnp.where(lower, pltpu.roll(xn, LANES - hd // 2, axis=1),
                                pltpu.roll(xn, hd // 2, axis=1))
            o_ref[:, h * LANES:(h + 1) * LANES] = ((xn * cs + partner * sn) * scale).astype(o_ref.dtype)

    prep(q_ref, qg_ref, qo_ref, ATT_HEAD_DIM ** -0.5)
    prep(k_ref, kg_ref, ko_ref, 1.0)


def _qkprep_call(proj, cs, sn, qg, kg, S, W, tm):
    N = proj.shape[0]
    tpb = S // tm
    return pl.pallas_call(
        _qkprep_kernel,
        out_shape=(jax.ShapeDtypeStruct((N, W), BF16), jax.ShapeDtypeStruct((N, W), BF16)),
        grid=(N // tm,),
        in_specs=[pl.BlockSpec((tm, W), lambda i: (i, 3)),
                  pl.BlockSpec((tm, W), lambda i: (i, 4)),
                  pl.BlockSpec((tm, LANES), lambda i: (i % tpb, 0)),
                  pl.BlockSpec((tm, LANES), lambda i: (i % tpb, 0)),
                  pl.BlockSpec((1, LANES), lambda i: (0, 0)),
                  pl.BlockSpec((1, LANES), lambda i: (0, 0))],
        out_specs=(pl.BlockSpec((tm, W), lambda i: (i, 0)), pl.BlockSpec((tm, W), lambda i: (i, 0))),
        compiler_params=_cparams(("arbitrary",)),
        name="qkprep",
    )(proj, proj, cs, sn, qg, kg)


def _rope_tables(S):
    hd = ATT_HEAD_DIM
    pos = jnp.arange(S, dtype=F32)
    inv = ROPE_THETA ** (-jnp.arange(0, hd, 2, dtype=F32) / hd)
    ang = pos[:, None] * inv[None, :]
    c, s = jnp.cos(ang), jnp.sin(ang)
    cs = jnp.concatenate([c, c, c, c], axis=1)
    sn = jnp.concatenate([-s, s, -s, s], axis=1)
    return cs, sn


def _attn_kernel(q_ref, k_ref, v_ref, lp_ref, sg_ref, o_ref, m_s, l_s, acc_s, *, tk, lam_init):
    tq = q_ref.shape[0]
    i = pl.program_id(2)
    lane = lax.broadcasted_iota(jnp.int32, (tq, LANES), 1)
    q = q_ref[...]
    zero = jnp.zeros_like(q)
    qc = (jnp.where(lane < ATT_HEAD_DIM, q, zero), jnp.where(lane < ATT_HEAD_DIM, zero, q))

    m_s[...] = jnp.full_like(m_s, -jnp.inf)
    l_s[...] = jnp.zeros_like(l_s)
    acc_s[...] = jnp.zeros_like(acc_s)

    def tile(j, masked):
        off = pl.multiple_of(j * tk, tk)
        k = k_ref[pl.ds(off, tk), :]
        v = v_ref[pl.ds(off, tk), :]
        for c in range(2):
            s = _dot_nt(qc[c], k)
            if masked:
                qpos = i * tq + lax.broadcasted_iota(jnp.int32, (tq, tk), 0)
                kpos = j * tk + lax.broadcasted_iota(jnp.int32, (tq, tk), 1)
                s = jnp.where(kpos <= qpos, s, -jnp.inf)
            m_old = m_s[c]
            m_new = jnp.maximum(m_old, jnp.max(s, axis=-1, keepdims=True))
            alpha = jnp.exp(m_old - m_new)
            p = jnp.exp(s - m_new)
            l_s[c] = alpha * l_s[c] + jnp.sum(p, axis=-1, keepdims=True)
            acc_s[c] = alpha * acc_s[c] + _dot(p.astype(BF16), v)
            m_s[c] = m_new

    n_full = (i * tq) // tk
    n_all = ((i + 1) * tq + tk - 1) // tk

    def body(j, carry):
        tile(j, False)
        return carry

    lax.fori_loop(0, n_full, body, 0)

    def body_m(j, carry):
        tile(j, True)
        return carry

    lax.fori_loop(n_full, n_all, body_m, 0)

    lp = lp_ref[...]
    lam = (jnp.exp(jnp.sum(lp[0:1] * lp[1:2], axis=-1, keepdims=True))
           - jnp.exp(jnp.sum(lp[2:3] * lp[3:4], axis=-1, keepdims=True)) + lam_init)
    o = acc_s[0] / l_s[0] - lam * (acc_s[1] / l_s[1])
    o = o * lax.rsqrt(jnp.mean(o * o, axis=-1, keepdims=True) + EPS) * sg_ref[...]
    o_ref[...] = (o * (1.0 - lam_init)).astype(o_ref.dtype)


def _attn_call(qr, kr, proj, lamp, sg, B, S, lam_init, tq, tk):
    N, W = qr.shape
    H = W // LANES
    nq = S // tq
    v_col0 = 5 * W // LANES
    return pl.pallas_call(
        functools.partial(_attn_kernel, tk=tk, lam_init=lam_init),
        out_shape=jax.ShapeDtypeStruct((N, W), BF16),
        grid=(B, H, nq),
        in_specs=[pl.BlockSpec((tq, LANES), lambda b, h, i: (b * nq + i, h)),
                  pl.BlockSpec((S, LANES), lambda b, h, i: (b, h)),
                  pl.BlockSpec((S, LANES), lambda b, h, i: (b, v_col0 + h)),
                  pl.BlockSpec((SUBLANES, LANES), lambda b, h, i: (0, 0)),
                  pl.BlockSpec((1, LANES), lambda b, h, i: (0, 0))],
        out_specs=pl.BlockSpec((tq, LANES), lambda b, h, i: (b * nq + i, h)),
        scratch_shapes=[pltpu.VMEM((2, tq, 1), F32), pltpu.VMEM((2, tq, 1), F32),
                        pltpu.VMEM((2, tq, LANES), F32)],
        compiler_params=_cparams(("arbitrary", "arbitrary", "arbitrary")),
        name="attn",
    )(qr, kr, proj, lamp, sg)


def _split_bf16(x):
    hi = x.astype(BF16)
    lo = (x - hi.astype(F32)).astype(BF16)
    return hi, lo


def _route(sel, aff):
    epg = EXPERTS_PER_GROUP
    T = sel.shape[1]
    rows = [sel[e:e + 1, :] for e in range(N_EXPERTS)]
    arow = [aff[e:e + 1, :] for e in range(N_EXPERTS)]
    best = None
    gidx = None
    for g in range(N_GROUPS):
        r = rows[g * epg:(g + 1) * epg]
        gs = None
        for a in range(epg):
            for b in range(a + 1, epg):
                pair = r[a] + r[b]
                gs = pair if gs is None else jnp.maximum(gs, pair)
        if best is None:
            best, gidx = gs, jnp.zeros((1, T), jnp.int32)
        else:
            better = gs > best
            gidx = jnp.where(better, g, gidx)
            best = jnp.where(better, gs, best)
    ig, ag = [], []
    for a in range(epg):
        s_a, f_a = rows[a], arow[a]
        for g in range(1, N_GROUPS):
            pick = gidx == g
            s_a = jnp.where(pick, rows[g * epg + a], s_a)
            f_a = jnp.where(pick, arow[g * epg + a], f_a)
        ig.append(s_a)
        ag.append(f_a)
    v1, a1, w1 = ig[0], jnp.zeros((1, T), jnp.int32), ag[0]
    for a in range(1, epg):
        better = ig[a] > v1
        a1 = jnp.where(better, a, a1)
        w1 = jnp.where(better, ag[a], w1)
        v1 = jnp.where(better, ig[a], v1)
    v2 = jnp.full((1, T), -jnp.inf, F32)
    a2 = jnp.zeros((1, T), jnp.int32)
    w2 = jnp.zeros((1, T), F32)
    for a in range(epg):
        better = jnp.logical_and(a1 != a, ig[a] > v2)
        a2 = jnp.where(better, a, a2)
        w2 = jnp.where(better, ag[a], w2)
        v2 = jnp.where(better, ig[a], v2)
    tot = w1 + w2
    e1 = gidx * epg + a1
    e2 = gidx * epg + a2
    erow = lax.broadcasted_iota(jnp.int32, (N_EXPERTS, T), 0)
    return jnp.where(erow == e1, w1 / tot, 0.0) + jnp.where(erow == e2, w2 / tot, 0.0)


def _merge_kernel(x_ref, bl_ref, bp_ref, ba_ref, gt_ref, mod_ref, wb_ref, wo_ref, g2_ref, rw_ref,
                  rb_ref, xo_ref, h2_ref, cb_ref):
    T, D = x_ref.shape
    m = mod_ref[0]
    merged = None
    for n, br in enumerate((bl_ref, bp_ref, ba_ref)):
        gate = jax.nn.sigmoid(gt_ref[:, n * D:(n + 1) * D].astype(F32))
        term = gate * _dot(br[...], wb_ref[n])
        merged = term if merged is None else merged + term
    y = _dot(merged.astype(BF16), wo_ref[...])
    x = x_ref[...] + m[2:3] * y
    xo_ref[...] = x
    var = jnp.mean(x * x, axis=-1, keepdims=True)
    h2 = x * lax.rsqrt(var + EPS) * g2_ref[...] * (1.0 + m[4:5]) + m[3:4]
    h2_ref[...] = h2.astype(BF16)
    h_hi, h_lo = _split_bf16(h2)
    rw = rw_ref[...]
    w_hi, w_lo = _split_bf16(rw)
    logits = _dot(h_hi, w_hi) + _dot(h_hi, w_lo) + _dot(h_lo, w_hi)
    lt = logits.T[0:N_EXPERTS, :]
    aff = jax.nn.sigmoid(lt)
    comb = _route(aff + rb_ref[...], aff)
    comb = jnp.concatenate([comb, jnp.zeros((LANES - N_EXPERTS, T), F32)], axis=0)
    cb_ref[...] = comb.T


def _merge_call(x, br_l, br_p, br_a, proj, mod, wb, wo, g2, rw, rb, S, tm):
    N, D = x.shape
    tpb = S // tm
    tok = lambda: pl.BlockSpec((tm, D), lambda i: (i, 0))
    gate_col = 6 * D // (N_BRANCH * D)
    return pl.pallas_call(
        _merge_kernel,
        out_shape=(jax.ShapeDtypeStruct((N, D), F32), jax.ShapeDtypeStruct((N, D), BF16),
                   jax.ShapeDtypeStruct((N, LANES), F32)),
        grid=(N // tm,),
        in_specs=[tok(), tok(), tok(), tok(),
                  pl.BlockSpec((tm, N_BRANCH * D), lambda i: (i, gate_col)),
                  pl.BlockSpec((1, SUBLANES, D), lambda i: (i // tpb, 0, 0)),
                  pl.BlockSpec((N_BRANCH, D, D), lambda i: (0, 0, 0)),
                  pl.BlockSpec((D, D), lambda i: (0, 0)),
                  pl.BlockSpec((1, D), lambda i: (0, 0)),
                  pl.BlockSpec((D, LANES), lambda i: (0, 0)),
                  pl.BlockSpec((N_EXPERTS, 1), lambda i: (0, 0))],
        out_specs=(tok(), tok(), pl.BlockSpec((tm, LANES), lambda i: (i, 0))),
        compiler_params=_cparams(("arbitrary",)),
        name="merge",
    )(x, br_l, br_p, br_a, proj, mod, wb, wo, g2, rw, rb)


def _moe_kernel(h_ref, cb_ref, x_ref, mod_ref, wg_ref, wu_ref, wd_ref, o_ref, acc):
    e = pl.program_id(1)

    @pl.when(e == 0)
    def _():
        acc[...] = jnp.zeros_like(acc)

    h = h_ref[...]
    g = _dot(h, wg_ref[0])
    u = _dot(h, wu_ref[0])
    cb = cb_ref[...]
    lane = lax.broadcasted_iota(jnp.int32, cb.shape, 1)
    cw = jnp.sum(jnp.where(lane == e, cb, 0.0), axis=-1, keepdims=True)
    act = (g * jax.nn.sigmoid(g)) * u * cw
    acc[...] += _dot(act.astype(BF16), wd_ref[0])

    @pl.when(e == pl.num_programs(1) - 1)
    def _():
        o_ref[...] = x_ref[...] + mod_ref[0][5:6] * acc[...]


def _moe_call(h2, comb, x, mod, wg, wu, wd, S, tm):
    N, D = x.shape
    E, _, F = wg.shape
    tpb = S // tm
    return pl.pallas_call(
        _moe_kernel,
        out_shape=jax.ShapeDtypeStruct((N, D), F32),
        grid=(N // tm, E),
        in_specs=[pl.BlockSpec((tm, D), lambda i, e: (i, 0)),
                  pl.BlockSpec((tm, LANES), lambda i, e: (i, 0)),
                  pl.BlockSpec((tm, D), lambda i, e: (i, 0)),
                  pl.BlockSpec((1, SUBLANES, D), lambda i, e: (i // tpb, 0, 0)),
                  pl.BlockSpec((1, D, F), lambda i, e: (e, 0, 0)),
                  pl.BlockSpec((1, D, F), lambda i, e: (e, 0, 0)),
                  pl.BlockSpec((1, F, D), lambda i, e: (e, 0, 0))],
        out_specs=pl.BlockSpec((tm, D), lambda i, e: (i, 0)),
        scratch_shapes=[pltpu.VMEM((tm, D), F32)],
        compiler_params=_cparams(("arbitrary", "arbitrary")),
        name="moe",
    )(h2, comb, x, mod, wg, wu, wd)


def _tiles(S):
    t = lambda want: min(want, S)
    return dict(inproj=t(512), lru=t(256), pool=t(256), qk=t(512), tq=t(512), tk=t(512),
                merge=t(512), moe=t(1024))


def kernel(x, c, w_ada, b_ada, norm1_g, norm2_g, w_in, conv_w, conv_b, lru_wa, lru_ba, lru_wx, lru_bx,
           lru_l, pool_w, pool_scale, qn_g, kn_g, lam_q1, lam_k1, lam_q2, lam_k2, subln_g, w_branch,
           w_out, router_w, router_b, w_gate, w_up, w_down):
    B, S, D = x.shape
    L = w_ada.shape[0]
    N = B * S
    W = D
    ts = _tiles(S)
    mod = _mod_call(c, w_ada, b_ada)
    cs, sn = _rope_tables(S)
    rw = jnp.pad(router_w, ((0, 0), (0, LANES - N_EXPERTS)))
    rb = router_b.reshape(N_EXPERTS, 1)
    hd = ATT_HEAD_DIM
    xf = x.reshape(N, D)
    for l in range(L):
        lam_init = 0.8 - 0.6 * math.exp(-0.3 * l)
        proj = _inproj_call(xf, mod[l], norm1_g[l].reshape(1, D), w_in[l].astype(BF16), S,
                            ts["inproj"], 3 * D)
        br_l = _lru_call(proj, conv_w[l], conv_b[l].reshape(1, W), lru_wa[l].astype(BF16),
                         lru_ba[l].reshape(1, W), lru_wx[l].astype(BF16), lru_bx[l].reshape(1, W),
                         lru_l[l].reshape(1, W), S, W, ts["lru"])
        br_p = _pool_call(proj, pool_w[l].astype(BF16), pool_scale[l].reshape(1, W), S, W, ts["pool"])
        qg = jnp.tile(qn_g[l], 2).reshape(1, LANES)
        kg = jnp.tile(kn_g[l], 2).reshape(1, LANES)
        qr, kr = _qkprep_call(proj, cs, sn, qg, kg, S, W, ts["qk"])
        lamp = jnp.zeros((SUBLANES, LANES), F32)
        lamp = lamp.at[0, :hd].set(lam_q1[l]).at[1, :hd].set(lam_k1[l])
        lamp = lamp.at[2, :hd].set(lam_q2[l]).at[3, :hd].set(lam_k2[l])
        br_a = _attn_call(qr, kr, proj, lamp, subln_g[l].reshape(1, LANES), B, S, lam_init,
                          ts["tq"], ts["tk"])
        xf, h2, comb = _merge_call(xf, br_l, br_p, br_a, proj, mod[l], w_branch[l].astype(BF16),
                                   w_out[l].astype(BF16), norm2_g[l].reshape(1, D), rw, rb, S,
                                   ts["merge"])
        xf = _moe_call(h2, comb, xf, mod[l], w_gate[l].astype(BF16), w_up[l].astype(BF16),
                       w_down[l].astype(BF16), S, ts["moe"])
    return xf.reshape(B, S, D)
```

```python
import functools
import math

import jax
import jax.numpy as jnp
from jax import lax
from jax.experimental import pallas as pl
from jax.experimental.pallas import tpu as pltpu

F32 = jnp.float32
BF16 = jnp.bfloat16

EPS = 1e-6
LANES = 128
SUBLANES = 8
LRU_BLOCKS = 8
CONV_WIDTH = 4
LRU_C = 8.0
POOL_WINDOWS = (2, 4, 8, 16)
POOL_HALO = 16
ATT_HEADS = 8
ATT_HEAD_DIM = 64
ROPE_THETA = 10000.0
N_BRANCH = 3
N_EXPERTS = 16
N_GROUPS = 4
EXPERTS_PER_GROUP = N_EXPERTS // N_GROUPS
VMEM_LIMIT = 56 * 1024 * 1024
LOG2E = math.log2(math.e)
SAFE_SCORE_BOUND = 32.0


def _cparams(sem):
    return pltpu.CompilerParams(dimension_semantics=sem, vmem_limit_bytes=VMEM_LIMIT)


def _dot(a, b):
    return jnp.dot(a, b, preferred_element_type=F32)


def _dot_nt(a, b):
    return lax.dot_general(a, b, (((1,), (1,)), ((), ())), preferred_element_type=F32)


def _mod_kernel(c_ref, w_ref, b_ref, o_ref):
    c = c_ref[...]
    ca = c * jax.nn.sigmoid(c)
    o_ref[0] = jnp.dot(ca, w_ref[0], preferred_element_type=F32,
                       precision=lax.Precision.HIGHEST) + b_ref[0]


def _mod_call(c, w_ada, b_ada):
    L, D, D6 = w_ada.shape
    B = c.shape[0]
    cp = jnp.zeros((SUBLANES, D), F32).at[:B].set(c)
    out = pl.pallas_call(
        _mod_kernel,
        out_shape=jax.ShapeDtypeStruct((L, SUBLANES, D6), F32),
        grid=(L, D6 // D),
        in_specs=[pl.BlockSpec((SUBLANES, D), lambda l, j: (0, 0)),
                  pl.BlockSpec((1, D, D), lambda l, j: (l, 0, j)),
                  pl.BlockSpec((1, 1, D), lambda l, j: (l, 0, j))],
        out_specs=pl.BlockSpec((1, SUBLANES, D), lambda l, j: (l, 0, j)),
        compiler_params=_cparams(("arbitrary", "arbitrary")),
        name="mod",
    )(cp, w_ada, b_ada.reshape(L, 1, D6))
    mod = out[:, :B].reshape(L, B, 6, D)
    return jnp.pad(mod, ((0, 0), (0, 0), (0, 2), (0, 0)))


def _inproj_kernel(x_ref, mod_ref, g_ref, w_ref, o_ref):
    x = x_ref[...]
    m = mod_ref[0]
    var = jnp.mean(x * x, axis=-1, keepdims=True)
    y = x * lax.rsqrt(var + EPS) * g_ref[...]
    h = y * (1.0 + m[1:2]) + m[0:1]
    o_ref[...] = _dot(h.astype(BF16), w_ref[...]).astype(BF16)


def _inproj_call(x, mod, g, w, S, tm, tn):
    N, D = x.shape
    W = w.shape[1]
    tpb = S // tm
    return pl.pallas_call(
        _inproj_kernel,
        out_shape=jax.ShapeDtypeStruct((N, W), BF16),
        grid=(W // tn, N // tm),
        in_specs=[pl.BlockSpec((tm, D), lambda j, i: (i, 0)),
                  pl.BlockSpec((1, SUBLANES, D), lambda j, i: (i // tpb, 0, 0)),
                  pl.BlockSpec((1, D), lambda j, i: (0, 0)),
                  pl.BlockSpec((D, tn), lambda j, i: (0, j))],
        out_specs=pl.BlockSpec((tm, tn), lambda j, i: (i, j)),
        compiler_params=_cparams(("arbitrary", "arbitrary")),
        name="inproj",
    )(x, mod, g, w)


def _lru_kernel(x_ref, y_ref, cw_ref, cb_ref, wa_ref, ba_ref, wx_ref, bx_ref, ll_ref, o_ref,
                xbuf, hcar, *, tpb):
    T, W = x_ref.shape
    blk = W // LRU_BLOCKS
    i = pl.program_id(0)

    @pl.when(i % tpb == 0)
    def _():
        xbuf[0:SUBLANES, :] = jnp.zeros((SUBLANES, W), F32)
        hcar[...] = jnp.zeros_like(hcar)

    xbuf[SUBLANES:SUBLANES + T, :] = x_ref[...].astype(F32)
    xc = cb_ref[...] + xbuf[SUBLANES:SUBLANES + T, :] * cw_ref[CONV_WIDTH - 1:CONV_WIDTH, :]
    for j in range(CONV_WIDTH - 1):
        d = CONV_WIDTH - 1 - j
        xc = xc + xbuf[SUBLANES - d:SUBLANES - d + T, :] * cw_ref[j:j + 1, :]
    xbuf[0:SUBLANES, :] = xbuf[T:T + SUBLANES, :]

    xcb = xc.astype(BF16)
    ga, gx = [], []
    for n in range(LRU_BLOCKS):
        xs = xcb[:, n * blk:(n + 1) * blk]
        ga.append(_dot(xs, wa_ref[n]))
        gx.append(_dot(xs, wx_ref[n]))
    gate_a = jax.nn.sigmoid(jnp.concatenate(ga, axis=1) + ba_ref[...])
    gate_x = jax.nn.sigmoid(jnp.concatenate(gx, axis=1) + bx_ref[...])
    ll = ll_ref[...]
    log_sig = jnp.minimum(ll, 0.0) - jnp.log1p(jnp.exp(-jnp.abs(ll)))
    log_a = LRU_C * gate_a * log_sig
    a = jnp.exp(log_a)
    u = jnp.sqrt(1.0 - a * a) * (gate_x * xc)

    row = lax.broadcasted_iota(jnp.int32, (T, W), 0)
    A, H = a, u
    sh = 1
    while sh < T:
        keep = row >= sh
        A_s = jnp.where(keep, pltpu.roll(A, sh, axis=0), 1.0)
        H_s = jnp.where(keep, pltpu.roll(H, sh, axis=0), 0.0)
        H = A * H_s + H
        A = A * A_s
        sh *= 2
    h = H + A * hcar[...]
    hcar[...] = h[T - 1:T, :]
    o_ref[...] = (h * jax.nn.gelu(y_ref[...].astype(F32))).astype(o_ref.dtype)


def _lru_call(proj, cw, cb, wa, ba, wx, bx, ll, S, W, tm):
    N = proj.shape[0]
    tpb = S // tm
    blk = W // LRU_BLOCKS
    vec = lambda: pl.BlockSpec((1, W), lambda i: (0, 0))
    return pl.pallas_call(
        functools.partial(_lru_kernel, tpb=tpb),
        out_shape=jax.ShapeDtypeStruct((N, W), BF16),
        grid=(N // tm,),
        in_specs=[pl.BlockSpec((tm, W), lambda i: (i, 0)),
                  pl.BlockSpec((tm, W), lambda i: (i, 1)),
                  pl.BlockSpec((CONV_WIDTH, W), lambda i: (0, 0)),
                  vec(),
                  pl.BlockSpec((LRU_BLOCKS, blk, blk), lambda i: (0, 0, 0)),
                  vec(),
                  pl.BlockSpec((LRU_BLOCKS, blk, blk), lambda i: (0, 0, 0)),
                  vec(),
                  vec()],
        out_specs=pl.BlockSpec((tm, W), lambda i: (i, 0)),
        scratch_shapes=[pltpu.VMEM((tm + SUBLANES, W), F32), pltpu.VMEM((1, W), F32)],
        compiler_params=_cparams(("arbitrary",)),
        name="lru",
    )(proj, proj, cw, cb, wa, ba, wx, bx, ll)


def _pool_kernel(x_ref, pw_ref, ps_ref, o_ref, xbuf, *, tpb):
    T, W = x_ref.shape
    G = len(POOL_WINDOWS)
    gw = W // G
    i = pl.program_id(0)

    @pl.when(i % tpb == 0)
    def _():
        xbuf[0:POOL_HALO, :] = jnp.zeros((POOL_HALO, W), F32)

    xbuf[POOL_HALO:POOL_HALO + T, :] = x_ref[...].astype(F32)
    t = (i % tpb) * T + lax.broadcasted_iota(jnp.int32, (T, 1), 0)
    outs = []
    for g, w in enumerate(POOL_WINDOWS):
        cols = slice(g * gw, (g + 1) * gw)
        xg = xbuf[POOL_HALO:POOL_HALO + T, cols]
        ws = xg
        for d in range(1, w):
            ws = ws + xbuf[POOL_HALO - d:POOL_HALO - d + T, cols]
        cnt = jnp.minimum(t + 1, w).astype(F32)
        pooled = (ws / cnt - xg).astype(BF16)
        outs.append(_dot(pooled, pw_ref[g]))
    xbuf[0:POOL_HALO, :] = xbuf[T:T + POOL_HALO, :]
    o_ref[...] = (jnp.concatenate(outs, axis=1) * ps_ref[...]).astype(o_ref.dtype)


def _pool_call(proj, pw, ps, S, W, tm):
    N = proj.shape[0]
    tpb = S // tm
    G = len(POOL_WINDOWS)
    return pl.pallas_call(
        functools.partial(_pool_kernel, tpb=tpb),
        out_shape=jax.ShapeDtypeStruct((N, W), BF16),
        grid=(N // tm,),
        in_specs=[pl.BlockSpec((tm, W), lambda i: (i, 2)),
                  pl.BlockSpec((G, W // G, W // G), lambda i: (0, 0, 0)),
                  pl.BlockSpec((1, W), lambda i: (0, 0))],
        out_specs=pl.BlockSpec((tm, W), lambda i: (i, 0)),
        scratch_shapes=[pltpu.VMEM((tm + POOL_HALO, W), F32)],
        compiler_params=_cparams(("arbitrary",)),
        name="pool",
    )(proj, pw, ps)


def _qkprep_kernel(q_ref, k_ref, cs_ref, sn_ref, qg_ref, kg_ref, qo_ref, ko_ref):
    T, W = q_ref.shape
    hd = ATT_HEAD_DIM
    lane = lax.broadcasted_iota(jnp.int32, (T, LANES), 1)
    first = lane < hd
    lower = (lane % hd) < (hd // 2)
    cs = cs_ref[...]
    sn = sn_ref[...]

    def prep(x_ref, g_ref, o_ref, scale):
        g = g_ref[...]
        for h in range(W // LANES):
            x = x_ref[:, h * LANES:(h + 1) * LANES].astype(F32)
            sq = x * x
            s1 = jnp.sum(jnp.where(first, sq, 0.0), axis=-1, keepdims=True)
            s2 = jnp.sum(jnp.where(first, 0.0, sq), axis=-1, keepdims=True)
            r = jnp.where(first, lax.rsqrt(s1 * (1.0 / hd) + EPS), lax.rsqrt(s2 * (1.0 / hd) + EPS))
            xn = x * r * g
            partner = jnp.where(lower, pltpu.roll(xn, LANES - hd // 2, axis=1),
                                pltpu.roll(xn, hd // 2, axis=1))
            o_ref[:, h * LANES:(h + 1) * LANES] = ((xn * cs + partner * sn) * scale).astype(o_ref.dtype)

    prep(q_ref, qg_ref, qo_ref, ATT_HEAD_DIM ** -0.5 * LOG2E)
    prep(k_ref, kg_ref, ko_ref, 1.0)


def _qkprep_call(proj, cs, sn, qg, kg, S, W, tm):
    N = proj.shape[0]
    tpb = S // tm
    return pl.pallas_call(
        _qkprep_kernel,
        out_shape=(jax.ShapeDtypeStruct((N, W), BF16), jax.ShapeDtypeStruct((N, W), BF16)),
        grid=(N // tm,),
        in_specs=[pl.BlockSpec((tm, W), lambda i: (i, 3)),
                  pl.BlockSpec((tm, W), lambda i: (i, 4)),
                  pl.BlockSpec((tm, LANES), lambda i: (i % tpb, 0)),
                  pl.BlockSpec((tm, LANES), lambda i: (i % tpb, 0)),
                  pl.BlockSpec((1, LANES), lambda i: (0, 0)),
                  pl.BlockSpec((1, LANES), lambda i: (0, 0))],
        out_specs=(pl.BlockSpec((tm, W), lambda i: (i, 0)), pl.BlockSpec((tm, W), lambda i: (i, 0))),
        compiler_params=_cparams(("arbitrary",)),
        name="qkprep",
    )(proj, proj, cs, sn, qg, kg)


def _rope_tables(S):
    hd = ATT_HEAD_DIM
    pos = jnp.arange(S, dtype=F32)
    inv = ROPE_THETA ** (-jnp.arange(0, hd, 2, dtype=F32) / hd)
    ang = pos[:, None] * inv[None, :]
    c, s = jnp.cos(ang), jnp.sin(ang)
    cs = jnp.concatenate([c, c, c, c], axis=1)
    sn = jnp.concatenate([-s, s, -s, s], axis=1)
    return cs, sn


def _attn_kernel(safe_ref, q_ref, k_ref, v_ref, lp_ref, sg_ref, o_ref, vx_s, m_s, l_s, acc_s, *,
                 tk, lam_init):
    tq = q_ref.shape[0]
    S = k_ref.shape[0]
    i = pl.program_id(2)
    lane = lax.broadcasted_iota(jnp.int32, (tq, LANES), 1)
    q = q_ref[...]
    zero = jnp.zeros_like(q)
    qc = (jnp.where(lane < ATT_HEAD_DIM, q, zero), jnp.where(lane < ATT_HEAD_DIM, zero, q))
    safe = safe_ref[0] == 1
    n_full = (i * tq) // tk
    n_all = ((i + 1) * tq + tk - 1) // tk

    def scores(c, j, k, masked):
        s = _dot_nt(qc[c], k)
        if masked:
            qpos = i * tq + lax.broadcasted_iota(jnp.int32, (tq, tk), 0)
            kpos = j * tk + lax.broadcasted_iota(jnp.int32, (tq, tk), 1)
            s = jnp.where(kpos <= qpos, s, -jnp.inf)
        return s

    def loops(tile):
        lax.fori_loop(0, n_full, lambda j, c: tile(j, False) or c, 0)
        lax.fori_loop(n_full, n_all, lambda j, c: tile(j, True) or c, 0)

    def finalize(a1, l1, a2, l2):
        lp = lp_ref[...]
        lam = (jnp.exp(jnp.sum(lp[0:1] * lp[1:2], axis=-1, keepdims=True))
               - jnp.exp(jnp.sum(lp[2:3] * lp[3:4], axis=-1, keepdims=True)) + lam_init)
        o = a1 / l1 - lam * (a2 / l2)
        o = o * lax.rsqrt(jnp.mean(o * o, axis=-1, keepdims=True) + EPS) * sg_ref[...]
        o_ref[...] = (o * (1.0 - lam_init)).astype(o_ref.dtype)

    @pl.when(jnp.logical_and(safe, i == 0))
    def _():
        ones_col = jnp.where(lax.broadcasted_iota(jnp.int32, (tk, LANES), 1) == 0, 1.0, 0.0).astype(BF16)

        def fill(j, c):
            off = pl.multiple_of(j * tk, tk)
            vx_s[pl.ds(off, tk), 0:LANES] = v_ref[pl.ds(off, tk), :]
            vx_s[pl.ds(off, tk), LANES:2 * LANES] = ones_col
            return c

        lax.fori_loop(0, S // tk, fill, 0)

    @pl.when(safe)
    def _():
        acc_s[...] = jnp.zeros_like(acc_s)

        def tile(j, masked):
            off = pl.multiple_of(j * tk, tk)
            k = k_ref[pl.ds(off, tk), :]
            vx = vx_s[pl.ds(off, tk), :]
            for c in range(2):
                p = jnp.exp2(scores(c, j, k, masked)).astype(BF16)
                acc_s[c] += _dot(p, vx)

        loops(tile)
        a1, a2 = acc_s[0], acc_s[1]
        finalize(a1[:, 0:LANES], a1[:, LANES:LANES + 1], a2[:, 0:LANES], a2[:, LANES:LANES + 1])

    @pl.when(jnp.logical_not(safe))
    def _():
        m_s[...] = jnp.full_like(m_s, -jnp.inf)
        l_s[...] = jnp.zeros_like(l_s)
        acc_s[...] = jnp.zeros_like(acc_s)

        def tile(j, masked):
            off = pl.multiple_of(j * tk, tk)
            k = k_ref[pl.ds(off, tk), :]
            v = v_ref[pl.ds(off, tk), :]
            for c in range(2):
                s = scores(c, j, k, masked)
                m_old = m_s[c]
                m_new = jnp.maximum(m_old, jnp.max(s, axis=-1, keepdims=True))
                alpha = jnp.exp2(m_old - m_new)
                p = jnp.exp2(s - m_new)
                l_s[c] = alpha * l_s[c] + jnp.sum(p, axis=-1, keepdims=True)
                acc_s[c, :, 0:LANES] = alpha * acc_s[c, :, 0:LANES] + _dot(p.astype(BF16), v)
                m_s[c] = m_new

        loops(tile)
        finalize(acc_s[0, :, 0:LANES], l_s[0], acc_s[1, :, 0:LANES], l_s[1])


def _attn_call(qr, kr, proj, safe, lamp, sg, B, S, lam_init, tq, tk):
    N, W = qr.shape
    H = W // LANES
    nq = S // tq
    v_col0 = 5 * W // LANES
    grid_spec = pltpu.PrefetchScalarGridSpec(
        num_scalar_prefetch=1,
        grid=(B, H, nq),
        in_specs=[pl.BlockSpec((tq, LANES), lambda b, h, i, f: (b * nq + i, h)),
                  pl.BlockSpec((S, LANES), lambda b, h, i, f: (b, h)),
                  pl.BlockSpec((S, LANES), lambda b, h, i, f: (b, v_col0 + h)),
                  pl.BlockSpec((SUBLANES, LANES), lambda b, h, i, f: (0, 0)),
                  pl.BlockSpec((1, LANES), lambda b, h, i, f: (0, 0))],
        out_specs=pl.BlockSpec((tq, LANES), lambda b, h, i, f: (b * nq + i, h)),
        scratch_shapes=[pltpu.VMEM((S, 2 * LANES), BF16),
                        pltpu.VMEM((2, tq, 1), F32), pltpu.VMEM((2, tq, 1), F32),
                        pltpu.VMEM((2, tq, 2 * LANES), F32)])
    return pl.pallas_call(
        functools.partial(_attn_kernel, tk=tk, lam_init=lam_init),
        out_shape=jax.ShapeDtypeStruct((N, W), BF16),
        grid_spec=grid_spec,
        compiler_params=_cparams(("arbitrary", "arbitrary", "arbitrary")),
        name="attn",
    )(safe, qr, kr, proj, lamp, sg)


def _split_bf16(x):
    hi = x.astype(BF16)
    lo = (x - hi.astype(F32)).astype(BF16)
    return hi, lo


def _route(sel, aff):
    epg = EXPERTS_PER_GROUP
    T = sel.shape[1]
    rows = [sel[e:e + 1, :] for e in range(N_EXPERTS)]
    arow = [aff[e:e + 1, :] for e in range(N_EXPERTS)]
    best = None
    gidx = None
    for g in range(N_GROUPS):
        r = rows[g * epg:(g + 1) * epg]
        gs = None
        for a in range(epg):
            for b in range(a + 1, epg):
                pair = r[a] + r[b]
                gs = pair if gs is None else jnp.maximum(gs, pair)
        if best is None:
            best, gidx = gs, jnp.zeros((1, T), jnp.int32)
        else:
            better = gs > best
            gidx = jnp.where(better, g, gidx)
            best = jnp.where(better, gs, best)
    ig, ag = [], []
    for a in range(epg):
        s_a, f_a = rows[a], arow[a]
        for g in range(1, N_GROUPS):
            pick = gidx == g
            s_a = jnp.where(pick, rows[g * epg + a], s_a)
            f_a = jnp.where(pick, arow[g * epg + a], f_a)
        ig.append(s_a)
        ag.append(f_a)
    v1, a1, w1 = ig[0], jnp.zeros((1, T), jnp.int32), ag[0]
    for a in range(1, epg):
        better = ig[a] > v1
        a1 = jnp.where(better, a, a1)
        w1 = jnp.where(better, ag[a], w1)
        v1 = jnp.where(better, ig[a], v1)
    v2 = jnp.full((1, T), -jnp.inf, F32)
    a2 = jnp.zeros((1, T), jnp.int32)
    w2 = jnp.zeros((1, T), F32)
    for a in range(epg):
        better = jnp.logical_and(a1 != a, ig[a] > v2)
        a2 = jnp.where(better, a, a2)
        w2 = jnp.where(better, ag[a], w2)
        v2 = jnp.where(better, ig[a], v2)
    tot = w1 + w2
    e1 = gidx * epg + a1
    e2 = gidx * epg + a2
    erow = lax.broadcasted_iota(jnp.int32, (N_EXPERTS, T), 0)
    return jnp.where(erow == e1, w1 / tot, 0.0) + jnp.where(erow == e2, w2 / tot, 0.0)


def _merge_kernel(x_ref, bl_ref, bp_ref, ba_ref, gt_ref, mod_ref, wb_ref, wo_ref, g2_ref, rw_ref,
                  rb_ref, xo_ref, h2_ref, cb_ref):
    T, D = x_ref.shape
    m = mod_ref[0]
    merged = None
    for n, br in enumerate((bl_ref, bp_ref, ba_ref)):
        gate = jax.nn.sigmoid(gt_ref[:, n * D:(n + 1) * D].astype(F32))
        term = gate * _dot(br[...], wb_ref[n])
        merged = term if merged is None else merged + term
    y = _dot(merged.astype(BF16), wo_ref[...])
    x = x_ref[...] + m[2:3] * y
    xo_ref[...] = x
    var = jnp.mean(x * x, axis=-1, keepdims=True)
    h2 = x * lax.rsqrt(var + EPS) * g2_ref[...] * (1.0 + m[4:5]) + m[3:4]
    h2_ref[...] = h2.astype(BF16)
    h_hi, h_lo = _split_bf16(h2)
    rw = rw_ref[...]
    w_hi, w_lo = _split_bf16(rw)
    logits = _dot(h_hi, w_hi) + _dot(h_hi, w_lo) + _dot(h_lo, w_hi)
    lt = logits.T[0:N_EXPERTS, :]
    aff = jax.nn.sigmoid(lt)
    comb = _route(aff + rb_ref[...], aff)
    comb = jnp.concatenate([comb, jnp.zeros((LANES - N_EXPERTS, T), F32)], axis=0)
    cb_ref[...] = comb.T


def _merge_call(x, br_l, br_p, br_a, proj, mod, wb, wo, g2, rw, rb, S, tm):
    N, D = x.shape
    tpb = S // tm
    tok = lambda: pl.BlockSpec((tm, D), lambda i: (i, 0))
    gate_col = 6 * D // (N_BRANCH * D)
    return pl.pallas_call(
        _merge_kernel,
        out_shape=(jax.ShapeDtypeStruct((N, D), F32), jax.ShapeDtypeStruct((N, D), BF16),
                   jax.ShapeDtypeStruct((N, LANES), F32)),
        grid=(N // tm,),
        in_specs=[tok(), tok(), tok(), tok(),
                  pl.BlockSpec((tm, N_BRANCH * D), lambda i: (i, gate_col)),
                  pl.BlockSpec((1, SUBLANES, D), lambda i: (i // tpb, 0, 0)),
                  pl.BlockSpec((N_BRANCH, D, D), lambda i: (0, 0, 0)),
                  pl.BlockSpec((D, D), lambda i: (0, 0)),
                  pl.BlockSpec((1, D), lambda i: (0, 0)),
                  pl.BlockSpec((D, LANES), lambda i: (0, 0)),
                  pl.BlockSpec((N_EXPERTS, 1), lambda i: (0, 0))],
        out_specs=(tok(), tok(), pl.BlockSpec((tm, LANES), lambda i: (i, 0))),
        compiler_params=_cparams(("arbitrary",)),
        name="merge",
    )(x, br_l, br_p, br_a, proj, mod, wb, wo, g2, rw, rb)


def _moe_kernel(h_ref, cb_ref, x_ref, mod_ref, wg_ref, wu_ref, wd_ref, o_ref, acc):
    e = pl.program_id(1)

    @pl.when(e == 0)
    def _():
        acc[...] = jnp.zeros_like(acc)

    h = h_ref[...]
    g = _dot(h, wg_ref[0])
    u = _dot(h, wu_ref[0])
    cb = cb_ref[...]
    lane = lax.broadcasted_iota(jnp.int32, cb.shape, 1)
    cw = jnp.sum(jnp.where(lane == e, cb, 0.0), axis=-1, keepdims=True)
    act = (g * jax.nn.sigmoid(g)) * u * cw
    acc[...] += _dot(act.astype(BF16), wd_ref[0])

    @pl.when(e == pl.num_programs(1) - 1)
    def _():
        o_ref[...] = x_ref[...] + mod_ref[0][5:6] * acc[...]


def _moe_call(h2, comb, x, mod, wg, wu, wd, S, tm):
    N, D = x.shape
    E, _, F = wg.shape
    tpb = S // tm
    return pl.pallas_call(
        _moe_kernel,
        out_shape=jax.ShapeDtypeStruct((N, D), F32),
        grid=(N // tm, E),
        in_specs=[pl.BlockSpec((tm, D), lambda i, e: (i, 0)),
                  pl.BlockSpec((tm, LANES), lambda i, e: (i, 0)),
                  pl.BlockSpec((tm, D), lambda i, e: (i, 0)),
                  pl.BlockSpec((1, SUBLANES, D), lambda i, e: (i // tpb, 0, 0)),
                  pl.BlockSpec((1, D, F), lambda i, e: (e, 0, 0)),
                  pl.BlockSpec((1, D, F), lambda i, e: (e, 0, 0)),
                  pl.BlockSpec((1, F, D), lambda i, e: (e, 0, 0))],
        out_specs=pl.BlockSpec((tm, D), lambda i, e: (i, 0)),
        scratch_shapes=[pltpu.VMEM((tm, D), F32)],
        compiler_params=_cparams(("arbitrary", "arbitrary")),
        name="moe",
    )(h2, comb, x, mod, wg, wu, wd)


def _tiles(S):
    t = lambda want: min(want, S)
    return dict(inproj=t(512), lru=t(256), pool=t(256), qk=t(512), tq=t(512), tk=t(512),
                merge=t(512), moe=t(1024))


def kernel(x, c, w_ada, b_ada, norm1_g, norm2_g, w_in, conv_w, conv_b, lru_wa, lru_ba, lru_wx, lru_bx,
           lru_l, pool_w, pool_scale, qn_g, kn_g, lam_q1, lam_k1, lam_q2, lam_k2, subln_g, w_branch,
           w_out, router_w, router_b, w_gate, w_up, w_down):
    B, S, D = x.shape
    L = w_ada.shape[0]
    N = B * S
    W = D
    ts = _tiles(S)
    mod = _mod_call(c, w_ada, b_ada)
    cs, sn = _rope_tables(S)
    rw = jnp.pad(router_w, ((0, 0), (0, LANES - N_EXPERTS)))
    rb = router_b.reshape(N_EXPERTS, 1)
    hd = ATT_HEAD_DIM
    xf = x.reshape(N, D)
    for l in range(L):
        lam_init = 0.8 - 0.6 * math.exp(-0.3 * l)
        proj = _inproj_call(xf, mod[l], norm1_g[l].reshape(1, D), w_in[l].astype(BF16), S,
                            ts["inproj"], 3 * D)
        br_l = _lru_call(proj, conv_w[l], conv_b[l].reshape(1, W), lru_wa[l].astype(BF16),
                         lru_ba[l].reshape(1, W), lru_wx[l].astype(BF16), lru_bx[l].reshape(1, W),
                         lru_l[l].reshape(1, W), S, W, ts["lru"])
        br_p = _pool_call(proj, pool_w[l].astype(BF16), pool_scale[l].reshape(1, W), S, W, ts["pool"])
        qg = jnp.tile(qn_g[l], 2).reshape(1, LANES)
        kg = jnp.tile(kn_g[l], 2).reshape(1, LANES)
        qr, kr = _qkprep_call(proj, cs, sn, qg, kg, S, W, ts["qk"])
        lamp = jnp.zeros((SUBLANES, LANES), F32)
        lamp = lamp.at[0, :hd].set(lam_q1[l]).at[1, :hd].set(lam_k1[l])
        lamp = lamp.at[2, :hd].set(lam_q2[l]).at[3, :hd].set(lam_k2[l])
        bound = math.sqrt(hd) * jnp.max(jnp.abs(qn_g[l])) * jnp.max(jnp.abs(kn_g[l]))
        safe = (bound < SAFE_SCORE_BOUND).astype(jnp.int32).reshape(1)
        br_a = _attn_call(qr, kr, proj, safe, lamp, subln_g[l].reshape(1, LANES), B, S, lam_init,
                          ts["tq"], ts["tk"])
        xf, h2, comb = _merge_call(xf, br_l, br_p, br_a, proj, mod[l], w_branch[l].astype(BF16),
                                   w_out[l].astype(BF16), norm2_g[l].reshape(1, D), rw, rb, S,
                                   ts["merge"])
        xf = _moe_call(h2, comb, xf, mod[l], w_gate[l].astype(BF16), w_up[l].astype(BF16),
                       w_down[l].astype(BF16), S, ts["moe"])
    return xf.reshape(B, S, D)
```

```python
import functools
import math

import jax
import jax.numpy as jnp
from jax import lax
from jax.experimental import pallas as pl
from jax.experimental.pallas import tpu as pltpu

F32 = jnp.float32
BF16 = jnp.bfloat16

EPS = 1e-6
LANES = 128
SUBLANES = 8
LRU_BLOCKS = 8
CONV_WIDTH = 4
LRU_C = 8.0
POOL_WINDOWS = (2, 4, 8, 16)
POOL_HALO = 16
ATT_HEADS = 8
ATT_HEAD_DIM = 64
ROPE_THETA = 10000.0
N_BRANCH = 3
N_EXPERTS = 16
N_GROUPS = 4
EXPERTS_PER_GROUP = N_EXPERTS // N_GROUPS
VMEM_LIMIT = 56 * 1024 * 1024
LOG2E = math.log2(math.e)
SAFE_SCORE_BOUND = 32.0


def _cparams(sem):
    return pltpu.CompilerParams(dimension_semantics=sem, vmem_limit_bytes=VMEM_LIMIT)


def _dot(a, b):
    return jnp.dot(a, b, preferred_element_type=F32)


def _dot_nt(a, b):
    return lax.dot_general(a, b, (((1,), (1,)), ((), ())), preferred_element_type=F32)


def _mod_kernel(c_ref, w_ref, b_ref, o_ref):
    c = c_ref[...]
    ca = c * jax.nn.sigmoid(c)
    o_ref[0] = jnp.dot(ca, w_ref[0], preferred_element_type=F32,
                       precision=lax.Precision.HIGHEST) + b_ref[0]


def _mod_call(c, w_ada, b_ada):
    L, D, D6 = w_ada.shape
    B = c.shape[0]
    cp = jnp.zeros((SUBLANES, D), F32).at[:B].set(c)
    out = pl.pallas_call(
        _mod_kernel,
        out_shape=jax.ShapeDtypeStruct((L, SUBLANES, D6), F32),
        grid=(L, D6 // D),
        in_specs=[pl.BlockSpec((SUBLANES, D), lambda l, j: (0, 0)),
                  pl.BlockSpec((1, D, D), lambda l, j: (l, 0, j)),
                  pl.BlockSpec((1, 1, D), lambda l, j: (l, 0, j))],
        out_specs=pl.BlockSpec((1, SUBLANES, D), lambda l, j: (l, 0, j)),
        compiler_params=_cparams(("arbitrary", "arbitrary")),
        name="mod",
    )(cp, w_ada, b_ada.reshape(L, 1, D6))
    mod = out[:, :B].reshape(L, B, 6, D)
    return jnp.pad(mod, ((0, 0), (0, 0), (0, 2), (0, 0)))


def _inproj_kernel(x_ref, mod_ref, g_ref, w_ref, o_ref):
    x = x_ref[...]
    m = mod_ref[0]
    var = jnp.mean(x * x, axis=-1, keepdims=True)
    y = x * lax.rsqrt(var + EPS) * g_ref[...]
    h = y * (1.0 + m[1:2]) + m[0:1]
    o_ref[...] = _dot(h.astype(BF16), w_ref[...]).astype(BF16)


def _inproj_call(x, mod, g, w, S, tm, tn):
    N, D = x.shape
    W = w.shape[1]
    tpb = S // tm
    return pl.pallas_call(
        _inproj_kernel,
        out_shape=jax.ShapeDtypeStruct((N, W), BF16),
        grid=(W // tn, N // tm),
        in_specs=[pl.BlockSpec((tm, D), lambda j, i: (i, 0)),
                  pl.BlockSpec((1, SUBLANES, D), lambda j, i: (i // tpb, 0, 0)),
                  pl.BlockSpec((1, D), lambda j, i: (0, 0)),
                  pl.BlockSpec((D, tn), lambda j, i: (0, j))],
        out_specs=pl.BlockSpec((tm, tn), lambda j, i: (i, j)),
        compiler_params=_cparams(("arbitrary", "arbitrary")),
        name="inproj",
    )(x, mod, g, w)


def _lru_kernel(x_ref, y_ref, cw_ref, cb_ref, wa_ref, ba_ref, wx_ref, bx_ref, ll_ref, o_ref,
                xbuf, hcar, *, tpb):
    T, W = x_ref.shape
    blk = W // LRU_BLOCKS
    i = pl.program_id(0)

    @pl.when(i % tpb == 0)
    def _():
        xbuf[0:SUBLANES, :] = jnp.zeros((SUBLANES, W), F32)
        hcar[...] = jnp.zeros_like(hcar)

    xbuf[SUBLANES:SUBLANES + T, :] = x_ref[...].astype(F32)
    xc = cb_ref[...] + xbuf[SUBLANES:SUBLANES + T, :] * cw_ref[CONV_WIDTH - 1:CONV_WIDTH, :]
    for j in range(CONV_WIDTH - 1):
        d = CONV_WIDTH - 1 - j
        xc = xc + xbuf[SUBLANES - d:SUBLANES - d + T, :] * cw_ref[j:j + 1, :]
    xbuf[0:SUBLANES, :] = xbuf[T:T + SUBLANES, :]

    xcb = xc.astype(BF16)
    ga, gx = [], []
    for n in range(LRU_BLOCKS):
        xs = xcb[:, n * blk:(n + 1) * blk]
        ga.append(_dot(xs, wa_ref[n]))
        gx.append(_dot(xs, wx_ref[n]))
    gate_a = jax.nn.sigmoid(jnp.concatenate(ga, axis=1) + ba_ref[...])
    gate_x = jax.nn.sigmoid(jnp.concatenate(gx, axis=1) + bx_ref[...])
    ll = ll_ref[...]
    log_sig = jnp.minimum(ll, 0.0) - jnp.log1p(jnp.exp(-jnp.abs(ll)))
    log_a = LRU_C * gate_a * log_sig
    a = jnp.exp(log_a)
    om = 1.0 - a * a
    u = om * lax.rsqrt(jnp.maximum(om, 1e-30)) * (gate_x * xc)

    G = T // SUBLANES
    A = a.reshape(G, SUBLANES, W)
    H = u.reshape(G, SUBLANES, W)
    sub = lax.broadcasted_iota(jnp.int32, (G, SUBLANES, W), 1)
    sh = 1
    while sh < SUBLANES:
        keep = sub >= sh
        A_s = jnp.where(keep, pltpu.roll(A, sh, axis=1), 1.0)
        H_s = jnp.where(keep, pltpu.roll(H, sh, axis=1), 0.0)
        H = A * H_s + H
        A = A * A_s
        sh *= 2
    gy = jax.nn.gelu(y_ref[...].astype(F32))
    carry = hcar[...]
    hs = []
    for g in range(G):
        hs.append(H[g] + A[g] * carry)
        carry = H[g, SUBLANES - 1:SUBLANES, :] + A[g, SUBLANES - 1:SUBLANES, :] * carry
        if g % 2 == 1:
            rows = slice((g - 1) * SUBLANES, (g + 1) * SUBLANES)
            h = jnp.concatenate(hs, axis=0)
            o_ref[rows, :] = (h * gy[rows, :]).astype(o_ref.dtype)
            hs = []
    hcar[...] = carry


def _lru_call(proj, cw, cb, wa, ba, wx, bx, ll, S, W, tm):
    N = proj.shape[0]
    tpb = S // tm
    blk = W // LRU_BLOCKS
    vec = lambda: pl.BlockSpec((1, W), lambda i: (0, 0))
    return pl.pallas_call(
        functools.partial(_lru_kernel, tpb=tpb),
        out_shape=jax.ShapeDtypeStruct((N, W), BF16),
        grid=(N // tm,),
        in_specs=[pl.BlockSpec((tm, W), lambda i: (i, 0)),
                  pl.BlockSpec((tm, W), lambda i: (i, 1)),
                  pl.BlockSpec((CONV_WIDTH, W), lambda i: (0, 0)),
                  vec(),
                  pl.BlockSpec((LRU_BLOCKS, blk, blk), lambda i: (0, 0, 0)),
                  vec(),
                  pl.BlockSpec((LRU_BLOCKS, blk, blk), lambda i: (0, 0, 0)),
                  vec(),
                  vec()],
        out_specs=pl.BlockSpec((tm, W), lambda i: (i, 0)),
        scratch_shapes=[pltpu.VMEM((tm + SUBLANES, W), F32), pltpu.VMEM((1, W), F32)],
        compiler_params=_cparams(("arbitrary",)),
        name="lru",
    )(proj, proj, cw, cb, wa, ba, wx, bx, ll)


def _pool_kernel(x_ref, pw_ref, ps_ref, o_ref, xbuf, *, tpb):
    T, W = x_ref.shape
    G = len(POOL_WINDOWS)
    gw = W // G
    i = pl.program_id(0)

    @pl.when(i % tpb == 0)
    def _():
        xbuf[0:POOL_HALO, :] = jnp.zeros((POOL_HALO, W), F32)

    xbuf[POOL_HALO:POOL_HALO + T, :] = x_ref[...].astype(F32)
    t = (i % tpb) * T + lax.broadcasted_iota(jnp.int32, (T, 1), 0)
    outs = []
    for g, w in enumerate(POOL_WINDOWS):
        cols = slice(g * gw, (g + 1) * gw)
        xg = xbuf[POOL_HALO:POOL_HALO + T, cols]
        ws = xg
        for d in range(1, w):
            ws = ws + xbuf[POOL_HALO - d:POOL_HALO - d + T, cols]
        cnt = jnp.minimum(t + 1, w).astype(F32)
        pooled = (ws / cnt - xg).astype(BF16)
        outs.append(_dot(pooled, pw_ref[g]))
    xbuf[0:POOL_HALO, :] = xbuf[T:T + POOL_HALO, :]
    o_ref[...] = (jnp.concatenate(outs, axis=1) * ps_ref[...]).astype(o_ref.dtype)


def _pool_call(proj, pw, ps, S, W, tm):
    N = proj.shape[0]
    tpb = S // tm
    G = len(POOL_WINDOWS)
    return pl.pallas_call(
        functools.partial(_pool_kernel, tpb=tpb),
        out_shape=jax.ShapeDtypeStruct((N, W), BF16),
        grid=(N // tm,),
        in_specs=[pl.BlockSpec((tm, W), lambda i: (i, 2)),
                  pl.BlockSpec((G, W // G, W // G), lambda i: (0, 0, 0)),
                  pl.BlockSpec((1, W), lambda i: (0, 0))],
        out_specs=pl.BlockSpec((tm, W), lambda i: (i, 0)),
        scratch_shapes=[pltpu.VMEM((tm + POOL_HALO, W), F32)],
        compiler_params=_cparams(("arbitrary",)),
        name="pool",
    )(proj, pw, ps)


def _qkprep_kernel(q_ref, k_ref, cs_ref, sn_ref, qg_ref, kg_ref, qo_ref, ko_ref):
    T, W = q_ref.shape
    hd = ATT_HEAD_DIM
    lane = lax.broadcasted_iota(jnp.int32, (T, LANES), 1)
    first = lane < hd
    lower = (lane % hd) < (hd // 2)
    cs = cs_ref[...]
    sn = sn_ref[...]

    def prep(x_ref, g_ref, o_ref, scale):
        g = g_ref[...]
        for h in range(W // LANES):
            x = x_ref[:, h * LANES:(h + 1) * LANES].astype(F32)
            sq = x * x
            s1 = jnp.sum(jnp.where(first, sq, 0.0), axis=-1, keepdims=True)
            s2 = jnp.sum(jnp.where(first, 0.0, sq), axis=-1, keepdims=True)
            r = jnp.where(first, lax.rsqrt(s1 * (1.0 / hd) + EPS), lax.rsqrt(s2 * (1.0 / hd) + EPS))
            xn = x * r * g
            partner = jnp.where(lower, pltpu.roll(xn, LANES - hd // 2, axis=1),
                                pltpu.roll(xn, hd // 2, axis=1))
            o_ref[:, h * LANES:(h + 1) * LANES] = ((xn * cs + partner * sn) * scale).astype(o_ref.dtype)

    prep(q_ref, qg_ref, qo_ref, ATT_HEAD_DIM ** -0.5 * LOG2E)
    prep(k_ref, kg_ref, ko_ref, 1.0)


def _qkprep_call(proj, cs, sn, qg, kg, S, W, tm):
    N = proj.shape[0]
    tpb = S // tm
    return pl.pallas_call(
        _qkprep_kernel,
        out_shape=(jax.ShapeDtypeStruct((N, W), BF16), jax.ShapeDtypeStruct((N, W), BF16)),
        grid=(N // tm,),
        in_specs=[pl.BlockSpec((tm, W), lambda i: (i, 3)),
                  pl.BlockSpec((tm, W), lambda i: (i, 4)),
                  pl.BlockSpec((tm, LANES), lambda i: (i % tpb, 0)),
                  pl.BlockSpec((tm, LANES), lambda i: (i % tpb, 0)),
                  pl.BlockSpec((1, LANES), lambda i: (0, 0)),
                  pl.BlockSpec((1, LANES), lambda i: (0, 0))],
        out_specs=(pl.BlockSpec((tm, W), lambda i: (i, 0)), pl.BlockSpec((tm, W), lambda i: (i, 0))),
        compiler_params=_cparams(("arbitrary",)),
        name="qkprep",
    )(proj, proj, cs, sn, qg, kg)


def _rope_tables(S):
    hd = ATT_HEAD_DIM
    pos = jnp.arange(S, dtype=F32)
    inv = ROPE_THETA ** (-jnp.arange(0, hd, 2, dtype=F32) / hd)
    ang = pos[:, None] * inv[None, :]
    c, s = jnp.cos(ang), jnp.sin(ang)
    cs = jnp.concatenate([c, c, c, c], axis=1)
    sn = jnp.concatenate([-s, s, -s, s], axis=1)
    return cs, sn


def _attn_kernel(safe_ref, q_ref, k_ref, v_ref, lp_ref, sg_ref, o_ref, vx_s, m_s, l_s, acc_s, *,
                 tk, lam_init):
    tq = q_ref.shape[0]
    S = k_ref.shape[0]
    i = pl.program_id(2)
    lane = lax.broadcasted_iota(jnp.int32, (tq, LANES), 1)
    q = q_ref[...]
    zero = jnp.zeros_like(q)
    qc = (jnp.where(lane < ATT_HEAD_DIM, q, zero), jnp.where(lane < ATT_HEAD_DIM, zero, q))
    safe = safe_ref[0] == 1
    n_full = (i * tq) // tk
    n_all = ((i + 1) * tq + tk - 1) // tk

    def scores(c, j, k, masked):
        s = _dot_nt(qc[c], k)
        if masked:
            qpos = i * tq + lax.broadcasted_iota(jnp.int32, (tq, tk), 0)
            kpos = j * tk + lax.broadcasted_iota(jnp.int32, (tq, tk), 1)
            s = jnp.where(kpos <= qpos, s, -jnp.inf)
        return s

    def loops(tile):
        lax.fori_loop(0, n_full, lambda j, c: tile(j, False) or c, 0)
        lax.fori_loop(n_full, n_all, lambda j, c: tile(j, True) or c, 0)

    def finalize(a1, l1, a2, l2):
        lp = lp_ref[...]
        lam = (jnp.exp(jnp.sum(lp[0:1] * lp[1:2], axis=-1, keepdims=True))
               - jnp.exp(jnp.sum(lp[2:3] * lp[3:4], axis=-1, keepdims=True)) + lam_init)
        o = a1 / l1 - lam * (a2 / l2)
        o = o * lax.rsqrt(jnp.mean(o * o, axis=-1, keepdims=True) + EPS) * sg_ref[...]
        o_ref[...] = (o * (1.0 - lam_init)).astype(o_ref.dtype)

    @pl.when(jnp.logical_and(safe, i == 0))
    def _():
        ones_col = jnp.where(lax.broadcasted_iota(jnp.int32, (tk, LANES), 1) == 0, 1.0, 0.0).astype(BF16)

        def fill(j, c):
            off = pl.multiple_of(j * tk, tk)
            vx_s[pl.ds(off, tk), 0:LANES] = v_ref[pl.ds(off, tk), :]
            vx_s[pl.ds(off, tk), LANES:2 * LANES] = ones_col
            return c

        lax.fori_loop(0, S // tk, fill, 0)

    @pl.when(safe)
    def _():
        acc_s[...] = jnp.zeros_like(acc_s)

        def tile(j, masked):
            off = pl.multiple_of(j * tk, tk)
            k = k_ref[pl.ds(off, tk), :]
            vx = vx_s[pl.ds(off, tk), :]
            for c in range(2):
                p = jnp.exp2(scores(c, j, k, masked)).astype(BF16)
                acc_s[c] += _dot(p, vx)

        loops(tile)
        a1, a2 = acc_s[0], acc_s[1]
        finalize(a1[:, 0:LANES], a1[:, LANES:LANES + 1], a2[:, 0:LANES], a2[:, LANES:LANES + 1])

    @pl.when(jnp.logical_not(safe))
    def _():
        m_s[...] = jnp.full_like(m_s, -jnp.inf)
        l_s[...] = jnp.zeros_like(l_s)
        acc_s[...] = jnp.zeros_like(acc_s)

        def tile(j, masked):
            off = pl.multiple_of(j * tk, tk)
            k = k_ref[pl.ds(off, tk), :]
            v = v_ref[pl.ds(off, tk), :]
            for c in range(2):
                s = scores(c, j, k, masked)
                m_old = m_s[c]
                m_new = jnp.maximum(m_old, jnp.max(s, axis=-1, keepdims=True))
                alpha = jnp.exp2(m_old - m_new)
                p = jnp.exp2(s - m_new)
                l_s[c] = alpha * l_s[c] + jnp.sum(p, axis=-1, keepdims=True)
                acc_s[c, :, 0:LANES] = alpha * acc_s[c, :, 0:LANES] + _dot(p.astype(BF16), v)
                m_s[c] = m_new

        loops(tile)
        finalize(acc_s[0, :, 0:LANES], l_s[0], acc_s[1, :, 0:LANES], l_s[1])


def _attn_call(qr, kr, proj, safe, lamp, sg, B, S, lam_init, tq, tk):
    N, W = qr.shape
    H = W // LANES
    nq = S // tq
    v_col0 = 5 * W // LANES
    grid_spec = pltpu.PrefetchScalarGridSpec(
        num_scalar_prefetch=1,
        grid=(B, H, nq),
        in_specs=[pl.BlockSpec((tq, LANES), lambda b, h, i, f: (b * nq + i, h)),
                  pl.BlockSpec((S, LANES), lambda b, h, i, f: (b, h)),
                  pl.BlockSpec((S, LANES), lambda b, h, i, f: (b, v_col0 + h)),
                  pl.BlockSpec((SUBLANES, LANES), lambda b, h, i, f: (0, 0)),
                  pl.BlockSpec((1, LANES), lambda b, h, i, f: (0, 0))],
        out_specs=pl.BlockSpec((tq, LANES), lambda b, h, i, f: (b * nq + i, h)),
        scratch_shapes=[pltpu.VMEM((S, 2 * LANES), BF16),
                        pltpu.VMEM((2, tq, 1), F32), pltpu.VMEM((2, tq, 1), F32),
                        pltpu.VMEM((2, tq, 2 * LANES), F32)])
    return pl.pallas_call(
        functools.partial(_attn_kernel, tk=tk, lam_init=lam_init),
        out_shape=jax.ShapeDtypeStruct((N, W), BF16),
        grid_spec=grid_spec,
        compiler_params=_cparams(("arbitrary", "arbitrary", "arbitrary")),
        name="attn",
    )(safe, qr, kr, proj, lamp, sg)


def _split_bf16(x):
    hi = x.astype(BF16)
    lo = (x - hi.astype(F32)).astype(BF16)
    return hi, lo


def _route(sel, aff):
    epg = EXPERTS_PER_GROUP
    T = sel.shape[1]
    rows = [sel[e:e + 1, :] for e in range(N_EXPERTS)]
    arow = [aff[e:e + 1, :] for e in range(N_EXPERTS)]
    best = None
    gidx = None
    for g in range(N_GROUPS):
        r = rows[g * epg:(g + 1) * epg]
        gs = None
        for a in range(epg):
            for b in range(a + 1, epg):
                pair = r[a] + r[b]
                gs = pair if gs is None else jnp.maximum(gs, pair)
        if best is None:
            best, gidx = gs, jnp.zeros((1, T), jnp.int32)
        else:
            better = gs > best
            gidx = jnp.where(better, g, gidx)
            best = jnp.where(better, gs, best)
    ig, ag = [], []
    for a in range(epg):
        s_a, f_a = rows[a], arow[a]
        for g in range(1, N_GROUPS):
            pick = gidx == g
            s_a = jnp.where(pick, rows[g * epg + a], s_a)
            f_a = jnp.where(pick, arow[g * epg + a], f_a)
        ig.append(s_a)
        ag.append(f_a)
    v1, a1, w1 = ig[0], jnp.zeros((1, T), jnp.int32), ag[0]
    for a in range(1, epg):
        better = ig[a] > v1
        a1 = jnp.where(better, a, a1)
        w1 = jnp.where(better, ag[a], w1)
        v1 = jnp.where(better, ig[a], v1)
    v2 = jnp.full((1, T), -jnp.inf, F32)
    a2 = jnp.zeros((1, T), jnp.int32)
    w2 = jnp.zeros((1, T), F32)
    for a in range(epg):
        better = jnp.logical_and(a1 != a, ig[a] > v2)
        a2 = jnp.where(better, a, a2)
        w2 = jnp.where(better, ag[a], w2)
        v2 = jnp.where(better, ig[a], v2)
    tot = w1 + w2
    e1 = gidx * epg + a1
    e2 = gidx * epg + a2
    erow = lax.broadcasted_iota(jnp.int32, (N_EXPERTS, T), 0)
    return jnp.where(erow == e1, w1 / tot, 0.0) + jnp.where(erow == e2, w2 / tot, 0.0)


def _merge_kernel(x_ref, bl_ref, bp_ref, ba_ref, gt_ref, mod_ref, wb_ref, wo_ref, g2_ref, rw_ref,
                  rb_ref, xo_ref, h2_ref, cb_ref):
    T, D = x_ref.shape
    m = mod_ref[0]
    merged = None
    for n, br in enumerate((bl_ref, bp_ref, ba_ref)):
        gate = jax.nn.sigmoid(gt_ref[:, n * D:(n + 1) * D].astype(F32))
        term = gate * _dot(br[...], wb_ref[n])
        merged = term if merged is None else merged + term
    y = _dot(merged.astype(BF16), wo_ref[...])
    x = x_ref[...] + m[2:3] * y
    xo_ref[...] = x
    var = jnp.mean(x * x, axis=-1, keepdims=True)
    h2 = x * lax.rsqrt(var + EPS) * g2_ref[...] * (1.0 + m[4:5]) + m[3:4]
    h2_ref[...] = h2.astype(BF16)
    h_hi, h_lo = _split_bf16(h2)
    rw = rw_ref[...]
    w_hi, w_lo = _split_bf16(rw)
    logits = _dot(h_hi, w_hi) + _dot(h_hi, w_lo) + _dot(h_lo, w_hi)
    lt = logits.T[0:N_EXPERTS, :]
    aff = jax.nn.sigmoid(lt)
    comb = _route(aff + rb_ref[...], aff)
    comb = jnp.concatenate([comb, jnp.zeros((LANES - N_EXPERTS, T), F32)], axis=0)
    cb_ref[...] = comb.T


def _merge_call(x, br_l, br_p, br_a, proj, mod, wb, wo, g2, rw, rb, S, tm):
    N, D = x.shape
    tpb = S // tm
    tok = lambda: pl.BlockSpec((tm, D), lambda i: (i, 0))
    gate_col = 6 * D // (N_BRANCH * D)
    return pl.pallas_call(
        _merge_kernel,
        out_shape=(jax.ShapeDtypeStruct((N, D), F32), jax.ShapeDtypeStruct((N, D), BF16),
                   jax.ShapeDtypeStruct((N, LANES), F32)),
        grid=(N // tm,),
        in_specs=[tok(), tok(), tok(), tok(),
                  pl.BlockSpec((tm, N_BRANCH * D), lambda i: (i, gate_col)),
                  pl.BlockSpec((1, SUBLANES, D), lambda i: (i // tpb, 0, 0)),
                  pl.BlockSpec((N_BRANCH, D, D), lambda i: (0, 0, 0)),
                  pl.BlockSpec((D, D), lambda i: (0, 0)),
                  pl.BlockSpec((1, D), lambda i: (0, 0)),
                  pl.BlockSpec((D, LANES), lambda i: (0, 0)),
                  pl.BlockSpec((N_EXPERTS, 1), lambda i: (0, 0))],
        out_specs=(tok(), tok(), pl.BlockSpec((tm, LANES), lambda i: (i, 0))),
        compiler_params=_cparams(("arbitrary",)),
        name="merge",
    )(x, br_l, br_p, br_a, proj, mod, wb, wo, g2, rw, rb)


def _moe_kernel(h_ref, cb_ref, x_ref, mod_ref, wg_ref, wu_ref, wd_ref, o_ref, acc):
    e = pl.program_id(1)

    @pl.when(e == 0)
    def _():
        acc[...] = jnp.zeros_like(acc)

    h = h_ref[...]
    g = _dot(h, wg_ref[0])
    u = _dot(h, wu_ref[0])
    cb = cb_ref[...]
    lane = lax.broadcasted_iota(jnp.int32, cb.shape, 1)
    cw = jnp.sum(jnp.where(lane == e, cb, 0.0), axis=-1, keepdims=True)
    act = (g * jax.nn.sigmoid(g)) * u * cw
    acc[...] += _dot(act.astype(BF16), wd_ref[0])

    @pl.when(e == pl.num_programs(1) - 1)
    def _():
        o_ref[...] = x_ref[...] + mod_ref[0][5:6] * acc[...]


def _moe_call(h2, comb, x, mod, wg, wu, wd, S, tm):
    N, D = x.shape
    E, _, F = wg.shape
    tpb = S // tm
    return pl.pallas_call(
        _moe_kernel,
        out_shape=jax.ShapeDtypeStruct((N, D), F32),
        grid=(N // tm, E),
        in_specs=[pl.BlockSpec((tm, D), lambda i, e: (i, 0)),
                  pl.BlockSpec((tm, LANES), lambda i, e: (i, 0)),
                  pl.BlockSpec((tm, D), lambda i, e: (i, 0)),
                  pl.BlockSpec((1, SUBLANES, D), lambda i, e: (i // tpb, 0, 0)),
                  pl.BlockSpec((1, D, F), lambda i, e: (e, 0, 0)),
                  pl.BlockSpec((1, D, F), lambda i, e: (e, 0, 0)),
                  pl.BlockSpec((1, F, D), lambda i, e: (e, 0, 0))],
        out_specs=pl.BlockSpec((tm, D), lambda i, e: (i, 0)),
        scratch_shapes=[pltpu.VMEM((tm, D), F32)],
        compiler_params=_cparams(("arbitrary", "arbitrary")),
        name="moe",
    )(h2, comb, x, mod, wg, wu, wd)


def _tiles(S):
    t = lambda want: min(want, S)
    return dict(inproj=t(512), lru=t(256), pool=t(256), qk=t(512), tq=t(1024), tk=t(1024),
                merge=t(512), moe=t(1024))


def kernel(x, c, w_ada, b_ada, norm1_g, norm2_g, w_in, conv_w, conv_b, lru_wa, lru_ba, lru_wx, lru_bx,
           lru_l, pool_w, pool_scale, qn_g, kn_g, lam_q1, lam_k1, lam_q2, lam_k2, subln_g, w_branch,
           w_out, router_w, router_b, w_gate, w_up, w_down):
    B, S, D = x.shape
    L = w_ada.shape[0]
    N = B * S
    W = D
    ts = _tiles(S)
    mod = _mod_call(c, w_ada, b_ada)
    cs, sn = _rope_tables(S)
    rw = jnp.pad(router_w, ((0, 0), (0, LANES - N_EXPERTS)))
    rb = router_b.reshape(N_EXPERTS, 1)
    hd = ATT_HEAD_DIM
    xf = x.reshape(N, D)
    for l in range(L):
        lam_init = 0.8 - 0.6 * math.exp(-0.3 * l)
        proj = _inproj_call(xf, mod[l], norm1_g[l].reshape(1, D), w_in[l].astype(BF16), S,
                            ts["inproj"], 3 * D)
        br_l = _lru_call(proj, conv_w[l], conv_b[l].reshape(1, W), lru_wa[l].astype(BF16),
                         lru_ba[l].reshape(1, W), lru_wx[l].astype(BF16), lru_bx[l].reshape(1, W),
                         lru_l[l].reshape(1, W), S, W, ts["lru"])
        br_p = _pool_call(proj, pool_w[l].astype(BF16), pool_scale[l].reshape(1, W), S, W, ts["pool"])
        qg = jnp.tile(qn_g[l], 2).reshape(1, LANES)
        kg = jnp.tile(kn_g[l], 2).reshape(1, LANES)
        qr, kr = _qkprep_call(proj, cs, sn, qg, kg, S, W, ts["qk"])
        lamp = jnp.zeros((SUBLANES, LANES), F32)
        lamp = lamp.at[0, :hd].set(lam_q1[l]).at[1, :hd].set(lam_k1[l])
        lamp = lamp.at[2, :hd].set(lam_q2[l]).at[3, :hd].set(lam_k2[l])
        bound = math.sqrt(hd) * jnp.max(jnp.abs(qn_g[l])) * jnp.max(jnp.abs(kn_g[l]))
        safe = (bound < SAFE_SCORE_BOUND).astype(jnp.int32).reshape(1)
        br_a = _attn_call(qr, kr, proj, safe, lamp, subln_g[l].reshape(1, LANES), B, S, lam_init,
                          ts["tq"], ts["tk"])
        xf, h2, comb = _merge_call(xf, br_l, br_p, br_a, proj, mod[l], w_branch[l].astype(BF16),
                                   w_out[l].astype(BF16), norm2_g[l].reshape(1, D), rw, rb, S,
                                   ts["merge"])
        xf = _moe_call(h2, comb, xf, mod[l], w_gate[l].astype(BF16), w_up[l].astype(BF16),
                       w_down[l].astype(BF16), S, ts["moe"])
    return xf.reshape(B, S, D)
```

```python
import functools
import math

import jax
import jax.numpy as jnp
from jax import lax
from jax.experimental import pallas as pl
from jax.experimental.pallas import tpu as pltpu

F32 = jnp.float32
BF16 = jnp.bfloat16

EPS = 1e-6
LANES = 128
SUBLANES = 8
LRU_BLOCKS = 8
CONV_WIDTH = 4
LRU_C = 8.0
POOL_WINDOWS = (2, 4, 8, 16)
POOL_HALO = 16
ATT_HEADS = 8
ATT_HEAD_DIM = 64
ROPE_THETA = 10000.0
N_BRANCH = 3
N_EXPERTS = 16
N_GROUPS = 4
EXPERTS_PER_GROUP = N_EXPERTS // N_GROUPS
MOE_CHUNK_ROWS = 128
MOE_RANK_BLOCK = 256
VMEM_LIMIT = 56 * 1024 * 1024
LOG2E = math.log2(math.e)
SAFE_SCORE_BOUND = 32.0


def _cparams(sem):
    return pltpu.CompilerParams(dimension_semantics=sem, vmem_limit_bytes=VMEM_LIMIT)


def _dot(a, b):
    return jnp.dot(a, b, preferred_element_type=F32)


def _dot_nt(a, b):
    return lax.dot_general(a, b, (((1,), (1,)), ((), ())), preferred_element_type=F32)


def _mod_kernel(c_ref, w_ref, b_ref, o_ref):
    c = c_ref[...]
    ca = c * jax.nn.sigmoid(c)
    o_ref[0] = jnp.dot(ca, w_ref[0], preferred_element_type=F32,
                       precision=lax.Precision.HIGHEST) + b_ref[0]


def _mod_call(c, w_ada, b_ada):
    L, D, D6 = w_ada.shape
    B = c.shape[0]
    cp = jnp.zeros((SUBLANES, D), F32).at[:B].set(c)
    out = pl.pallas_call(
        _mod_kernel,
        out_shape=jax.ShapeDtypeStruct((L, SUBLANES, D6), F32),
        grid=(L, D6 // D),
        in_specs=[pl.BlockSpec((SUBLANES, D), lambda l, j: (0, 0)),
                  pl.BlockSpec((1, D, D), lambda l, j: (l, 0, j)),
                  pl.BlockSpec((1, 1, D), lambda l, j: (l, 0, j))],
        out_specs=pl.BlockSpec((1, SUBLANES, D), lambda l, j: (l, 0, j)),
        compiler_params=_cparams(("arbitrary", "arbitrary")),
        name="mod",
    )(cp, w_ada, b_ada.reshape(L, 1, D6))
    mod = out[:, :B].reshape(L, B, 6, D)
    return jnp.pad(mod, ((0, 0), (0, 0), (0, 2), (0, 0)))


def _qk_norm_rope(x, g, cs, sn, first, lower, scale):
    hd = ATT_HEAD_DIM
    sq = x * x
    s1 = jnp.sum(jnp.where(first, sq, 0.0), axis=-1, keepdims=True)
    s2 = jnp.sum(jnp.where(first, 0.0, sq), axis=-1, keepdims=True)
    r = jnp.where(first, lax.rsqrt(s1 * (1.0 / hd) + EPS), lax.rsqrt(s2 * (1.0 / hd) + EPS))
    xn = x * r * g
    partner = jnp.where(lower, pltpu.roll(xn, LANES - hd // 2, axis=1), pltpu.roll(xn, hd // 2, axis=1))
    return (xn * cs + partner * sn) * scale


def _inproj_kernel(x_ref, mod_ref, g_ref, w_ref, *rest, qk_cols):
    o_ref = rest[-1]
    x = x_ref[...]
    m = mod_ref[0]
    var = jnp.mean(x * x, axis=-1, keepdims=True)
    y = x * lax.rsqrt(var + EPS) * g_ref[...]
    h = y * (1.0 + m[1:2]) + m[0:1]
    res = _dot(h.astype(BF16), w_ref[...])
    if not qk_cols:
        o_ref[...] = res.astype(BF16)
        return
    cs_ref, sn_ref, qg_ref, kg_ref = rest[:4]
    T = x.shape[0]
    lane = lax.broadcasted_iota(jnp.int32, (T, LANES), 1)
    first = lane < ATT_HEAD_DIM
    lower = (lane % ATT_HEAD_DIM) < (ATT_HEAD_DIM // 2)
    cs, sn = cs_ref[...], sn_ref[...]
    half = qk_cols // 2
    for c in range(qk_cols // LANES):
        cols = slice(c * LANES, (c + 1) * LANES)
        is_q = c * LANES < half
        g = qg_ref[...] if is_q else kg_ref[...]
        scale = ATT_HEAD_DIM ** -0.5 * LOG2E if is_q else 1.0
        o_ref[:, cols] = _qk_norm_rope(res[:, cols], g, cs, sn, first, lower, scale).astype(BF16)
    o_ref[:, qk_cols:] = res[:, qk_cols:].astype(BF16)


def _inproj_call(x, mod, g, w, S, tm, rope=None):
    N, D = x.shape
    W = w.shape[1]
    tpb = S // tm
    in_specs = [pl.BlockSpec((tm, D), lambda i: (i, 0)),
                pl.BlockSpec((1, SUBLANES, D), lambda i: (i // tpb, 0, 0)),
                pl.BlockSpec((1, D), lambda i: (0, 0)),
                pl.BlockSpec((D, W), lambda i: (0, 0))]
    args = [x, mod, g, w]
    if rope is not None:
        in_specs += [pl.BlockSpec((tm, LANES), lambda i: (i % tpb, 0)),
                     pl.BlockSpec((tm, LANES), lambda i: (i % tpb, 0)),
                     pl.BlockSpec((1, LANES), lambda i: (0, 0)),
                     pl.BlockSpec((1, LANES), lambda i: (0, 0))]
        args += list(rope)
    return pl.pallas_call(
        functools.partial(_inproj_kernel, qk_cols=0 if rope is None else 2 * W // 3),
        out_shape=jax.ShapeDtypeStruct((N, W), BF16),
        grid=(N // tm,),
        in_specs=in_specs,
        out_specs=pl.BlockSpec((tm, W), lambda i: (i, 0)),
        compiler_params=_cparams(("arbitrary",)),
        name="inproj_qkv" if rope is not None else "inproj",
    )(*args)


def _lru_kernel(x_ref, y_ref, cw_ref, cb_ref, wa_ref, ba_ref, wx_ref, bx_ref, ll_ref, o_ref,
                xbuf, hcar, *, tpb):
    T, W = x_ref.shape
    blk = W // LRU_BLOCKS
    i = pl.program_id(0)

    @pl.when(i % tpb == 0)
    def _():
        xbuf[0:SUBLANES, :] = jnp.zeros((SUBLANES, W), F32)
        hcar[...] = jnp.zeros_like(hcar)

    xbuf[SUBLANES:SUBLANES + T, :] = x_ref[...].astype(F32)
    xc = cb_ref[...] + xbuf[SUBLANES:SUBLANES + T, :] * cw_ref[CONV_WIDTH - 1:CONV_WIDTH, :]
    for j in range(CONV_WIDTH - 1):
        d = CONV_WIDTH - 1 - j
        xc = xc + xbuf[SUBLANES - d:SUBLANES - d + T, :] * cw_ref[j:j + 1, :]
    xbuf[0:SUBLANES, :] = xbuf[T:T + SUBLANES, :]

    xcb = xc.astype(BF16)
    ga, gx = [], []
    for n in range(LRU_BLOCKS):
        xs = xcb[:, n * blk:(n + 1) * blk]
        ga.append(_dot(xs, wa_ref[n]))
        gx.append(_dot(xs, wx_ref[n]))
    gate_a = jax.nn.sigmoid(jnp.concatenate(ga, axis=1) + ba_ref[...])
    gate_x = jax.nn.sigmoid(jnp.concatenate(gx, axis=1) + bx_ref[...])
    ll = ll_ref[...]
    log_sig = jnp.minimum(ll, 0.0) - jnp.log1p(jnp.exp(-jnp.abs(ll)))
    log_a = LRU_C * gate_a * log_sig
    a = jnp.exp(log_a)
    om = 1.0 - a * a
    u = om * lax.rsqrt(jnp.maximum(om, 1e-30)) * (gate_x * xc)

    G = T // SUBLANES
    A = a.reshape(G, SUBLANES, W)
    H = u.reshape(G, SUBLANES, W)
    sub = lax.broadcasted_iota(jnp.int32, (G, SUBLANES, W), 1)
    sh = 1
    while sh < SUBLANES:
        keep = sub >= sh
        A_s = jnp.where(keep, pltpu.roll(A, sh, axis=1), 1.0)
        H_s = jnp.where(keep, pltpu.roll(H, sh, axis=1), 0.0)
        H = A * H_s + H
        A = A * A_s
        sh *= 2
    gy = jax.nn.gelu(y_ref[...].astype(F32))
    carry = hcar[...]
    hs = []
    for g in range(G):
        hs.append(H[g] + A[g] * carry)
        carry = H[g, SUBLANES - 1:SUBLANES, :] + A[g, SUBLANES - 1:SUBLANES, :] * carry
        if g % 2 == 1:
            rows = slice((g - 1) * SUBLANES, (g + 1) * SUBLANES)
            h = jnp.concatenate(hs, axis=0)
            o_ref[rows, :] = (h * gy[rows, :]).astype(o_ref.dtype)
            hs = []
    hcar[...] = carry


def _lru_call(proj, cw, cb, wa, ba, wx, bx, ll, S, W, tm):
    N = proj.shape[0]
    tpb = S // tm
    blk = W // LRU_BLOCKS
    vec = lambda: pl.BlockSpec((1, W), lambda i: (0, 0))
    return pl.pallas_call(
        functools.partial(_lru_kernel, tpb=tpb),
        out_shape=jax.ShapeDtypeStruct((N, W), BF16),
        grid=(N // tm,),
        in_specs=[pl.BlockSpec((tm, W), lambda i: (i, 0)),
                  pl.BlockSpec((tm, W), lambda i: (i, 1)),
                  pl.BlockSpec((CONV_WIDTH, W), lambda i: (0, 0)),
                  vec(),
                  pl.BlockSpec((LRU_BLOCKS, blk, blk), lambda i: (0, 0, 0)),
                  vec(),
                  pl.BlockSpec((LRU_BLOCKS, blk, blk), lambda i: (0, 0, 0)),
                  vec(),
                  vec()],
        out_specs=pl.BlockSpec((tm, W), lambda i: (i, 0)),
        scratch_shapes=[pltpu.VMEM((tm + SUBLANES, W), F32), pltpu.VMEM((1, W), F32)],
        compiler_params=_cparams(("arbitrary",)),
        name="lru",
    )(proj, proj, cw, cb, wa, ba, wx, bx, ll)


def _pool_kernel(x_ref, pw_ref, ps_ref, o_ref, xbuf, *, tpb):
    T, W = x_ref.shape
    G = len(POOL_WINDOWS)
    gw = W // G
    i = pl.program_id(0)

    @pl.when(i % tpb == 0)
    def _():
        xbuf[0:POOL_HALO, :] = jnp.zeros((POOL_HALO, W), F32)

    xbuf[POOL_HALO:POOL_HALO + T, :] = x_ref[...].astype(F32)
    t = (i % tpb) * T + lax.broadcasted_iota(jnp.int32, (T, 1), 0)
    outs = []
    for g, w in enumerate(POOL_WINDOWS):
        cols = slice(g * gw, (g + 1) * gw)
        xg = xbuf[POOL_HALO:POOL_HALO + T, cols]
        ws = xg
        for d in range(1, w):
            ws = ws + xbuf[POOL_HALO - d:POOL_HALO - d + T, cols]
        cnt = jnp.minimum(t + 1, w).astype(F32)
        pooled = (ws / cnt - xg).astype(BF16)
        outs.append(_dot(pooled, pw_ref[g]))
    xbuf[0:POOL_HALO, :] = xbuf[T:T + POOL_HALO, :]
    o_ref[...] = (jnp.concatenate(outs, axis=1) * ps_ref[...]).astype(o_ref.dtype)


def _pool_call(proj, pw, ps, S, W, tm):
    N = proj.shape[0]
    tpb = S // tm
    G = len(POOL_WINDOWS)
    return pl.pallas_call(
        functools.partial(_pool_kernel, tpb=tpb),
        out_shape=jax.ShapeDtypeStruct((N, W), BF16),
        grid=(N // tm,),
        in_specs=[pl.BlockSpec((tm, W), lambda i: (i, 2)),
                  pl.BlockSpec((G, W // G, W // G), lambda i: (0, 0, 0)),
                  pl.BlockSpec((1, W), lambda i: (0, 0))],
        out_specs=pl.BlockSpec((tm, W), lambda i: (i, 0)),
        scratch_shapes=[pltpu.VMEM((tm + POOL_HALO, W), F32)],
        compiler_params=_cparams(("arbitrary",)),
        name="pool",
    )(proj, pw, ps)


def _rope_tables(S):
    hd = ATT_HEAD_DIM
    pos = jnp.arange(S, dtype=F32)
    inv = ROPE_THETA ** (-jnp.arange(0, hd, 2, dtype=F32) / hd)
    ang = pos[:, None] * inv[None, :]
    c, s = jnp.cos(ang), jnp.sin(ang)
    cs = jnp.concatenate([c, c, c, c], axis=1)
    sn = jnp.concatenate([-s, s, -s, s], axis=1)
    return cs, sn


def _attn_kernel(safe_ref, q_ref, k_ref, v_ref, lp_ref, sg_ref, o_ref, vx_s, m_s, l_s, acc_s, *,
                 tk, lam_init):
    tq = q_ref.shape[0]
    S = k_ref.shape[0]
    i = pl.program_id(2)
    lane = lax.broadcasted_iota(jnp.int32, (tq, LANES), 1)
    q = q_ref[...]
    zero = jnp.zeros_like(q)
    qc = (jnp.where(lane < ATT_HEAD_DIM, q, zero), jnp.where(lane < ATT_HEAD_DIM, zero, q))
    safe = safe_ref[0] == 1
    n_full = (i * tq) // tk
    n_all = ((i + 1) * tq + tk - 1) // tk

    def scores(c, j, k, masked):
        s = _dot_nt(qc[c], k)
        if masked:
            qpos = i * tq + lax.broadcasted_iota(jnp.int32, (tq, tk), 0)
            kpos = j * tk + lax.broadcasted_iota(jnp.int32, (tq, tk), 1)
            s = jnp.where(kpos <= qpos, s, -jnp.inf)
        return s

    def loops(tile):
        lax.fori_loop(0, n_full, lambda j, c: tile(j, False) or c, 0)
        lax.fori_loop(n_full, n_all, lambda j, c: tile(j, True) or c, 0)

    def finalize(a1, l1, a2, l2):
        lp = lp_ref[...]
        lam = (jnp.exp(jnp.sum(lp[0:1] * lp[1:2], axis=-1, keepdims=True))
               - jnp.exp(jnp.sum(lp[2:3] * lp[3:4], axis=-1, keepdims=True)) + lam_init)
        o = a1 / l1 - lam * (a2 / l2)
        o = o * lax.rsqrt(jnp.mean(o * o, axis=-1, keepdims=True) + EPS) * sg_ref[...]
        o_ref[...] = (o * (1.0 - lam_init)).astype(o_ref.dtype)

    @pl.when(jnp.logical_and(safe, i == 0))
    def _():
        ones_col = jnp.where(lax.broadcasted_iota(jnp.int32, (tk, LANES), 1) == 0, 1.0, 0.0).astype(BF16)

        def fill(j, c):
            off = pl.multiple_of(j * tk, tk)
            vx_s[pl.ds(off, tk), 0:LANES] = v_ref[pl.ds(off, tk), :]
            vx_s[pl.ds(off, tk), LANES:2 * LANES] = ones_col
            return c

        lax.fori_loop(0, S // tk, fill, 0)

    @pl.when(safe)
    def _():
        acc_s[...] = jnp.zeros_like(acc_s)

        def tile(j, masked):
            off = pl.multiple_of(j * tk, tk)
            k = k_ref[pl.ds(off, tk), :]
            vx = vx_s[pl.ds(off, tk), :]
            if masked and tq == tk:
                hq = tq // 2
                tri = (lax.broadcasted_iota(jnp.int32, (hq, hq), 1)
                       <= lax.broadcasted_iota(jnp.int32, (hq, hq), 0))
                for c in range(2):
                    s_top = jnp.where(tri, _dot_nt(qc[c][0:hq], k[0:hq]), -jnp.inf)
                    acc_s[c, 0:hq, :] += _dot(jnp.exp2(s_top).astype(BF16), vx[0:hq])
                    s_lo = _dot_nt(qc[c][hq:tq], k)
                    s_lo = jnp.concatenate([s_lo[:, 0:hq], jnp.where(tri, s_lo[:, hq:tk], -jnp.inf)], axis=1)
                    acc_s[c, hq:tq, :] += _dot(jnp.exp2(s_lo).astype(BF16), vx)
                return
            for c in range(2):
                p = jnp.exp2(scores(c, j, k, masked)).astype(BF16)
                acc_s[c] += _dot(p, vx)

        loops(tile)
        a1, a2 = acc_s[0], acc_s[1]
        finalize(a1[:, 0:LANES], a1[:, LANES:LANES + 1], a2[:, 0:LANES], a2[:, LANES:LANES + 1])

    @pl.when(jnp.logical_not(safe))
    def _():
        m_s[...] = jnp.full_like(m_s, -jnp.inf)
        l_s[...] = jnp.zeros_like(l_s)
        acc_s[...] = jnp.zeros_like(acc_s)

        def tile(j, masked):
            off = pl.multiple_of(j * tk, tk)
            k = k_ref[pl.ds(off, tk), :]
            v = v_ref[pl.ds(off, tk), :]
            for c in range(2):
                s = scores(c, j, k, masked)
                m_old = m_s[c]
                m_new = jnp.maximum(m_old, jnp.max(s, axis=-1, keepdims=True))
                alpha = jnp.exp2(m_old - m_new)
                p = jnp.exp2(s - m_new)
                l_s[c] = alpha * l_s[c] + jnp.sum(p, axis=-1, keepdims=True)
                acc_s[c, :, 0:LANES] = alpha * acc_s[c, :, 0:LANES] + _dot(p.astype(BF16), v)
                m_s[c] = m_new

        loops(tile)
        finalize(acc_s[0, :, 0:LANES], l_s[0], acc_s[1, :, 0:LANES], l_s[1])


def _attn_call(qkv, safe, lamp, sg, B, S, lam_init, tq, tk):
    N = qkv.shape[0]
    W = qkv.shape[1] // 3
    H = W // LANES
    nq = S // tq
    grid_spec = pltpu.PrefetchScalarGridSpec(
        num_scalar_prefetch=1,
        grid=(B, H, nq),
        in_specs=[pl.BlockSpec((tq, LANES), lambda b, h, i, f: (b * nq + i, h)),
                  pl.BlockSpec((S, LANES), lambda b, h, i, f: (b, H + h)),
                  pl.BlockSpec((S, LANES), lambda b, h, i, f: (b, 2 * H + h)),
                  pl.BlockSpec((SUBLANES, LANES), lambda b, h, i, f: (0, 0)),
                  pl.BlockSpec((1, LANES), lambda b, h, i, f: (0, 0))],
        out_specs=pl.BlockSpec((tq, LANES), lambda b, h, i, f: (b * nq + i, h)),
        scratch_shapes=[pltpu.VMEM((S, 2 * LANES), BF16),
                        pltpu.VMEM((2, tq, 1), F32), pltpu.VMEM((2, tq, 1), F32),
                        pltpu.VMEM((2, tq, 2 * LANES), F32)])
    return pl.pallas_call(
        functools.partial(_attn_kernel, tk=tk, lam_init=lam_init),
        out_shape=jax.ShapeDtypeStruct((N, W), BF16),
        grid_spec=grid_spec,
        compiler_params=_cparams(("arbitrary", "arbitrary", "arbitrary")),
        name="attn",
    )(safe, qkv, qkv, qkv, lamp, sg)


def _split_bf16(x):
    hi = x.astype(BF16)
    lo = (x - hi.astype(F32)).astype(BF16)
    return hi, lo


def _route(sel, aff):
    epg = EXPERTS_PER_GROUP
    T = sel.shape[1]
    rows = [sel[e:e + 1, :] for e in range(N_EXPERTS)]
    arow = [aff[e:e + 1, :] for e in range(N_EXPERTS)]
    best = None
    gidx = None
    for g in range(N_GROUPS):
        r = rows[g * epg:(g + 1) * epg]
        gs = None
        for a in range(epg):
            for b in range(a + 1, epg):
                pair = r[a] + r[b]
                gs = pair if gs is None else jnp.maximum(gs, pair)
        if best is None:
            best, gidx = gs, jnp.zeros((1, T), jnp.int32)
        else:
            better = gs > best
            gidx = jnp.where(better, g, gidx)
            best = jnp.where(better, gs, best)
    ig, ag = [], []
    for a in range(epg):
        s_a, f_a = rows[a], arow[a]
        for g in range(1, N_GROUPS):
            pick = gidx == g
            s_a = jnp.where(pick, rows[g * epg + a], s_a)
            f_a = jnp.where(pick, arow[g * epg + a], f_a)
        ig.append(s_a)
        ag.append(f_a)
    v1, a1, w1 = ig[0], jnp.zeros((1, T), jnp.int32), ag[0]
    for a in range(1, epg):
        better = ig[a] > v1
        a1 = jnp.where(better, a, a1)
        w1 = jnp.where(better, ag[a], w1)
        v1 = jnp.where(better, ig[a], v1)
    v2 = jnp.full((1, T), -jnp.inf, F32)
    a2 = jnp.zeros((1, T), jnp.int32)
    w2 = jnp.zeros((1, T), F32)
    for a in range(epg):
        better = jnp.logical_and(a1 != a, ig[a] > v2)
        a2 = jnp.where(better, a, a2)
        w2 = jnp.where(better, ag[a], w2)
        v2 = jnp.where(better, ig[a], v2)
    tot = w1 + w2
    e1 = gidx * epg + a1
    e2 = gidx * epg + a2
    erow = lax.broadcasted_iota(jnp.int32, (N_EXPERTS, T), 0)
    comb = jnp.where(erow == e1, w1 / tot, 0.0) + jnp.where(erow == e2, w2 / tot, 0.0)
    return comb, gidx


def _merge_kernel(x_ref, bl_ref, bp_ref, ba_ref, gt_ref, mod_ref, wb_ref, wo_ref, g2_ref, rw_ref,
                  rb_ref, xo_ref, h2_ref, cb_ref, gr_ref):
    T, D = x_ref.shape
    m = mod_ref[0]
    merged = None
    for n, br in enumerate((bl_ref, bp_ref, ba_ref)):
        gate = jax.nn.sigmoid(gt_ref[:, n * D:(n + 1) * D].astype(F32))
        term = gate * _dot(br[...], wb_ref[n])
        merged = term if merged is None else merged + term
    y = _dot(merged.astype(BF16), wo_ref[...])
    x = x_ref[...] + m[2:3] * y
    xo_ref[...] = x
    var = jnp.mean(x * x, axis=-1, keepdims=True)
    h2 = x * lax.rsqrt(var + EPS) * g2_ref[...] * (1.0 + m[4:5]) + m[3:4]
    h2_ref[...] = h2.astype(BF16)
    h_hi, h_lo = _split_bf16(h2)
    rw = rw_ref[...]
    w_hi, w_lo = _split_bf16(rw)
    logits = _dot(h_hi, w_hi) + _dot(h_hi, w_lo) + _dot(h_lo, w_hi)
    lt = logits.T[0:N_EXPERTS, :]
    aff = jax.nn.sigmoid(lt)
    comb, gidx = _route(aff + rb_ref[...], aff)
    comb = jnp.concatenate([comb, jnp.zeros((LANES - N_EXPERTS, T), F32)], axis=0)
    cb_ref[...] = comb.T
    gr_ref[...] = jnp.broadcast_to(gidx.astype(F32), (SUBLANES, T))


def _merge_call(x, br_l, br_p, br_a, gates, mod, wb, wo, g2, rw, rb, S, tm):
    N, D = x.shape
    tpb = S // tm
    tok = lambda: pl.BlockSpec((tm, D), lambda i: (i, 0))
    return pl.pallas_call(
        _merge_kernel,
        out_shape=(jax.ShapeDtypeStruct((N, D), F32), jax.ShapeDtypeStruct((N, D), BF16),
                   jax.ShapeDtypeStruct((N, LANES), F32), jax.ShapeDtypeStruct((SUBLANES, N), F32)),
        grid=(N // tm,),
        in_specs=[tok(), tok(), tok(), tok(),
                  pl.BlockSpec((tm, N_BRANCH * D), lambda i: (i, 0)),
                  pl.BlockSpec((1, SUBLANES, D), lambda i: (i // tpb, 0, 0)),
                  pl.BlockSpec((N_BRANCH, D, D), lambda i: (0, 0, 0)),
                  pl.BlockSpec((D, D), lambda i: (0, 0)),
                  pl.BlockSpec((1, D), lambda i: (0, 0)),
                  pl.BlockSpec((D, LANES), lambda i: (0, 0)),
                  pl.BlockSpec((N_EXPERTS, 1), lambda i: (0, 0))],
        out_specs=(tok(), tok(), pl.BlockSpec((tm, LANES), lambda i: (i, 0)),
                   pl.BlockSpec((SUBLANES, tm), lambda i: (0, i))),
        compiler_params=_cparams(("arbitrary",)),
        name="merge",
    )(x, br_l, br_p, br_a, gates, mod, wb, wo, g2, rw, rb)


def _moe_kernel(h_ref, cb_ref, gr_ref, x_ref, mod_ref, wg_ref, wu_ref, wd_ref, o_ref,
                dm_s, xs_s, cw_s, ys_s, seg_s):
    T, D = h_ref.shape
    R = MOE_CHUNK_ROWS
    e = pl.program_id(1)

    @pl.when(e == 0)
    def _():
        gid = gr_ref[0:1, :]
        grow = lax.broadcasted_iota(jnp.int32, (2 * SUBLANES, T), 0).astype(F32)
        oh = grow == gid
        ohf = jnp.where(oh, 1.0, 0.0)
        ohb = ohf.astype(BF16)
        rb = MOE_RANK_BLOCK
        tri = jnp.where(lax.broadcasted_iota(jnp.int32, (rb, rb), 0)
                        < lax.broadcasted_iota(jnp.int32, (rb, rb), 1), 1.0, 0.0).astype(BF16)
        count = jnp.zeros((2 * SUBLANES, 1), F32)
        ranks = []
        for b in range(T // rb):
            cols = slice(b * rb, (b + 1) * rb)
            ranks.append(_dot(ohb[:, cols], tri) + count)
            count = count + jnp.sum(ohf[:, cols], axis=1, keepdims=True)
        rank = jnp.concatenate(ranks, axis=1)
        rowi = lax.broadcasted_iota(jnp.int32, (2 * SUBLANES, 1), 0)
        start = jnp.zeros((2 * SUBLANES, 1), F32)
        run = jnp.zeros((1, 1), F32)
        for g in range(N_GROUPS):
            start = jnp.where(rowi == g, run, start)
            seg_s[g] = run[0, 0].astype(jnp.int32)
            run = run + count[g:g + 1, :]
        seg_s[N_GROUPS] = run[0, 0].astype(jnp.int32)
        pos = jnp.sum(jnp.where(oh, start + rank, 0.0), axis=0, keepdims=True).astype(jnp.int32)
        dm = jnp.where(lax.broadcasted_iota(jnp.int32, (T, T), 0) == pos, 1.0, 0.0).astype(BF16)
        dm_s[...] = dm
        xs_s[...] = _dot(dm, h_ref[...]).astype(BF16)
        c_hi, c_lo = _split_bf16(cb_ref[...])
        cw_s[...] = _dot(dm, c_hi) + _dot(dm, c_lo)
        ys_s[...] = jnp.zeros_like(ys_s)

    g = e // EXPERTS_PER_GROUP
    lo = seg_s[g]
    hi = seg_s[g + 1]
    lane = lax.broadcasted_iota(jnp.int32, (R, LANES), 1)

    def chunk(c, carry):
        rows = pl.ds(pl.multiple_of(c * R, R), R)
        xc = xs_s[rows, :]
        gg = _dot(xc, wg_ref[0])
        uu = _dot(xc, wu_ref[0])
        cw = jnp.sum(jnp.where(lane == e, cw_s[rows, :], 0.0), axis=-1, keepdims=True)
        act = (gg * jax.nn.sigmoid(gg)) * uu * cw
        ys_s[rows, :] += _dot(act.astype(BF16), wd_ref[0])
        return carry

    lax.fori_loop(lo // R, (hi + R - 1) // R, chunk, 0)

    @pl.when(e == pl.num_programs(1) - 1)
    def _():
        y = lax.dot_general(dm_s[...], ys_s[...].astype(BF16), (((0,), (0,)), ((), ())),
                            preferred_element_type=F32)
        o_ref[...] = x_ref[...] + mod_ref[0][5:6] * y


def _moe_call(h2, comb, grp, x, mod, wg, wu, wd, S, tm):
    N, D = x.shape
    E, _, F = wg.shape
    tpb = S // tm
    return pl.pallas_call(
        _moe_kernel,
        out_shape=jax.ShapeDtypeStruct((N, D), F32),
        grid=(N // tm, E),
        in_specs=[pl.BlockSpec((tm, D), lambda i, e: (i, 0)),
                  pl.BlockSpec((tm, LANES), lambda i, e: (i, 0)),
                  pl.BlockSpec((SUBLANES, tm), lambda i, e: (0, i)),
                  pl.BlockSpec((tm, D), lambda i, e: (i, 0)),
                  pl.BlockSpec((1, SUBLANES, D), lambda i, e: (i // tpb, 0, 0)),
                  pl.BlockSpec((1, D, F), lambda i, e: (e, 0, 0)),
                  pl.BlockSpec((1, D, F), lambda i, e: (e, 0, 0)),
                  pl.BlockSpec((1, F, D), lambda i, e: (e, 0, 0))],
        out_specs=pl.BlockSpec((tm, D), lambda i, e: (i, 0)),
        scratch_shapes=[pltpu.VMEM((tm, tm), BF16), pltpu.VMEM((tm, D), BF16),
                        pltpu.VMEM((tm, LANES), F32), pltpu.VMEM((tm, D), F32),
                        pltpu.SMEM((SUBLANES,), jnp.int32)],
        compiler_params=_cparams(("arbitrary", "arbitrary")),
        name="moe",
    )(h2, comb, grp, x, mod, wg, wu, wd)


def _tiles(S):
    t = lambda want: min(want, S)
    return dict(inproj=t(512), lru=t(256), pool=t(256), qk=t(512), tq=t(1024), tk=t(1024),
                merge=t(512), moe=t(1024))


def kernel(x, c, w_ada, b_ada, norm1_g, norm2_g, w_in, conv_w, conv_b, lru_wa, lru_ba, lru_wx, lru_bx,
           lru_l, pool_w, pool_scale, qn_g, kn_g, lam_q1, lam_k1, lam_q2, lam_k2, subln_g, w_branch,
           w_out, router_w, router_b, w_gate, w_up, w_down):
    B, S, D = x.shape
    L = w_ada.shape[0]
    N = B * S
    W = D
    ts = _tiles(S)
    mod = _mod_call(c, w_ada, b_ada)
    cs, sn = _rope_tables(S)
    rw = jnp.pad(router_w, ((0, 0), (0, LANES - N_EXPERTS)))
    rb = router_b.reshape(N_EXPERTS, 1)
    hd = ATT_HEAD_DIM
    xf = x.reshape(N, D)
    for l in range(L):
        lam_init = 0.8 - 0.6 * math.exp(-0.3 * l)
        w_l = w_in[l].astype(BF16)
        g1 = norm1_g[l].reshape(1, D)
        qg = jnp.tile(qn_g[l], 2).reshape(1, LANES)
        kg = jnp.tile(kn_g[l], 2).reshape(1, LANES)
        proj = _inproj_call(xf, mod[l], g1, w_l[:, 0:3 * D], S, ts["inproj"])
        qkv = _inproj_call(xf, mod[l], g1, w_l[:, 3 * D:6 * D], S, ts["inproj"], rope=(cs, sn, qg, kg))
        gates = _inproj_call(xf, mod[l], g1, w_l[:, 6 * D:9 * D], S, ts["inproj"])
        br_l = _lru_call(proj, conv_w[l], conv_b[l].reshape(1, W), lru_wa[l].astype(BF16),
                         lru_ba[l].reshape(1, W), lru_wx[l].astype(BF16), lru_bx[l].reshape(1, W),
                         lru_l[l].reshape(1, W), S, W, ts["lru"])
        br_p = _pool_call(proj, pool_w[l].astype(BF16), pool_scale[l].reshape(1, W), S, W, ts["pool"])
        lamp = jnp.zeros((SUBLANES, LANES), F32)
        lamp = lamp.at[0, :hd].set(lam_q1[l]).at[1, :hd].set(lam_k1[l])
        lamp = lamp.at[2, :hd].set(lam_q2[l]).at[3, :hd].set(lam_k2[l])
        bound = math.sqrt(hd) * jnp.max(jnp.abs(qn_g[l])) * jnp.max(jnp.abs(kn_g[l]))
        safe = (bound < SAFE_SCORE_BOUND).astype(jnp.int32).reshape(1)
        br_a = _attn_call(qkv, safe, lamp, subln_g[l].reshape(1, LANES), B, S, lam_init,
                          ts["tq"], ts["tk"])
        xf, h2, comb, grp = _merge_call(xf, br_l, br_p, br_a, gates, mod[l], w_branch[l].astype(BF16),
                                        w_out[l].astype(BF16), norm2_g[l].reshape(1, D), rw, rb, S,
                                        ts["merge"])
        xf = _moe_call(h2, comb, grp, xf, mod[l], w_gate[l].astype(BF16), w_up[l].astype(BF16),
                       w_down[l].astype(BF16), S, ts["moe"])
    return xf.reshape(B, S, D)
```

```python
import functools
import math

import jax
import jax.numpy as jnp
from jax import lax
from jax.experimental import pallas as pl
from jax.experimental.pallas import tpu as pltpu

F32 = jnp.float32
BF16 = jnp.bfloat16

EPS = 1e-6
LANES = 128
SUBLANES = 8
LRU_BLOCKS = 8
CONV_WIDTH = 4
LRU_C = 8.0
POOL_WINDOWS = (2, 4, 8, 16)
POOL_HALO = 16
ATT_HEADS = 8
ATT_HEAD_DIM = 64
ATT_VT_ROWS = 144
ROPE_THETA = 10000.0
N_BRANCH = 3
N_EXPERTS = 16
N_GROUPS = 4
EXPERTS_PER_GROUP = N_EXPERTS // N_GROUPS
MOE_CHUNK_ROWS = 128
MOE_EXPERTS_PER_STEP = 2
MOE_RANK_BLOCK = 256
VMEM_LIMIT = 56 * 1024 * 1024
LOG2E = math.log2(math.e)
SAFE_SCORE_BOUND = 32.0


def _cparams(sem):
    return pltpu.CompilerParams(dimension_semantics=sem, vmem_limit_bytes=VMEM_LIMIT)


def _dot(a, b):
    return jnp.dot(a, b, preferred_element_type=F32)


def _dot_nt(a, b):
    return lax.dot_general(a, b, (((1,), (1,)), ((), ())), preferred_element_type=F32)


def _mod_kernel(c_ref, w_ref, b_ref, o_ref):
    c = c_ref[...]
    ca = c * jax.nn.sigmoid(c)
    o_ref[0] = jnp.dot(ca, w_ref[0], preferred_element_type=F32,
                       precision=lax.Precision.HIGHEST) + b_ref[0]


def _mod_call(c, w_ada, b_ada):
    L, D, D6 = w_ada.shape
    B = c.shape[0]
    cp = jnp.zeros((SUBLANES, D), F32).at[:B].set(c)
    out = pl.pallas_call(
        _mod_kernel,
        out_shape=jax.ShapeDtypeStruct((L, SUBLANES, D6), F32),
        grid=(L, D6 // D),
        in_specs=[pl.BlockSpec((SUBLANES, D), lambda l, j: (0, 0)),
                  pl.BlockSpec((1, D, D), lambda l, j: (l, 0, j)),
                  pl.BlockSpec((1, 1, D), lambda l, j: (l, 0, j))],
        out_specs=pl.BlockSpec((1, SUBLANES, D), lambda l, j: (l, 0, j)),
        compiler_params=_cparams(("arbitrary", "arbitrary")),
        name="mod",
    )(cp, w_ada, b_ada.reshape(L, 1, D6))
    mod = out[:, :B].reshape(L, B, 6, D)
    return jnp.pad(mod, ((0, 0), (0, 0), (0, 2), (0, 0)))


def _qk_norm_rope(x, g, cs, sn, first, lower, scale):
    hd = ATT_HEAD_DIM
    sq = x * x
    s1 = jnp.sum(jnp.where(first, sq, 0.0), axis=-1, keepdims=True)
    s2 = jnp.sum(jnp.where(first, 0.0, sq), axis=-1, keepdims=True)
    r = jnp.where(first, lax.rsqrt(s1 * (1.0 / hd) + EPS), lax.rsqrt(s2 * (1.0 / hd) + EPS))
    xn = x * r * g
    partner = jnp.where(lower, pltpu.roll(xn, LANES - hd // 2, axis=1), pltpu.roll(xn, hd // 2, axis=1))
    return (xn * cs + partner * sn) * scale


def _inproj_kernel(x_ref, mod_ref, g_ref, w_ref, *rest, qk_cols):
    o_ref = rest[-1]
    x = x_ref[...]
    m = mod_ref[0]
    var = jnp.mean(x * x, axis=-1, keepdims=True)
    y = x * lax.rsqrt(var + EPS) * g_ref[...]
    h = y * (1.0 + m[1:2]) + m[0:1]
    res = _dot(h.astype(BF16), w_ref[...])
    if not qk_cols:
        o_ref[...] = res.astype(BF16)
        return
    cs_ref, sn_ref, qg_ref, kg_ref = rest[:4]
    T = x.shape[0]
    lane = lax.broadcasted_iota(jnp.int32, (T, LANES), 1)
    first = lane < ATT_HEAD_DIM
    lower = (lane % ATT_HEAD_DIM) < (ATT_HEAD_DIM // 2)
    cs, sn = cs_ref[...], sn_ref[...]
    half = qk_cols // 2
    for c in range(qk_cols // LANES):
        cols = slice(c * LANES, (c + 1) * LANES)
        is_q = c * LANES < half
        g = qg_ref[...] if is_q else kg_ref[...]
        scale = ATT_HEAD_DIM ** -0.5 * LOG2E if is_q else 1.0
        o_ref[:, cols] = _qk_norm_rope(res[:, cols], g, cs, sn, first, lower, scale).astype(BF16)
    o_ref[:, qk_cols:] = res[:, qk_cols:].astype(BF16)


def _inproj_call(x, mod, g, w, S, tm, rope=None):
    N, D = x.shape
    W = w.shape[1]
    tpb = S // tm
    in_specs = [pl.BlockSpec((tm, D), lambda i: (i, 0)),
                pl.BlockSpec((1, SUBLANES, D), lambda i: (i // tpb, 0, 0)),
                pl.BlockSpec((1, D), lambda i: (0, 0)),
                pl.BlockSpec((D, W), lambda i: (0, 0))]
    args = [x, mod, g, w]
    if rope is not None:
        in_specs += [pl.BlockSpec((tm, LANES), lambda i: (i % tpb, 0)),
                     pl.BlockSpec((tm, LANES), lambda i: (i % tpb, 0)),
                     pl.BlockSpec((1, LANES), lambda i: (0, 0)),
                     pl.BlockSpec((1, LANES), lambda i: (0, 0))]
        args += list(rope)
    return pl.pallas_call(
        functools.partial(_inproj_kernel, qk_cols=0 if rope is None else 2 * W // 3),
        out_shape=jax.ShapeDtypeStruct((N, W), BF16),
        grid=(N // tm,),
        in_specs=in_specs,
        out_specs=pl.BlockSpec((tm, W), lambda i: (i, 0)),
        compiler_params=_cparams(("arbitrary",)),
        name="inproj_qkv" if rope is not None else "inproj",
    )(*args)


def _lru_kernel(x_ref, y_ref, cw_ref, cb_ref, wa_ref, ba_ref, wx_ref, bx_ref, ll_ref, o_ref,
                xbuf, hcar, *, tpb):
    T, W = x_ref.shape
    blk = W // LRU_BLOCKS
    i = pl.program_id(0)

    @pl.when(i % tpb == 0)
    def _():
        xbuf[0:SUBLANES, :] = jnp.zeros((SUBLANES, W), F32)
        hcar[...] = jnp.zeros_like(hcar)

    xbuf[SUBLANES:SUBLANES + T, :] = x_ref[...].astype(F32)
    xc = cb_ref[...] + xbuf[SUBLANES:SUBLANES + T, :] * cw_ref[CONV_WIDTH - 1:CONV_WIDTH, :]
    for j in range(CONV_WIDTH - 1):
        d = CONV_WIDTH - 1 - j
        xc = xc + xbuf[SUBLANES - d:SUBLANES - d + T, :] * cw_ref[j:j + 1, :]
    xbuf[0:SUBLANES, :] = xbuf[T:T + SUBLANES, :]

    xcb = xc.astype(BF16)
    ga, gx = [], []
    for n in range(LRU_BLOCKS):
        xs = xcb[:, n * blk:(n + 1) * blk]
        ga.append(_dot(xs, wa_ref[n]))
        gx.append(_dot(xs, wx_ref[n]))
    gate_a = jax.nn.sigmoid(jnp.concatenate(ga, axis=1) + ba_ref[...])
    gate_x = jax.nn.sigmoid(jnp.concatenate(gx, axis=1) + bx_ref[...])
    ll = ll_ref[...]
    log_sig = jnp.minimum(ll, 0.0) - jnp.log1p(jnp.exp(-jnp.abs(ll)))
    log_a = LRU_C * gate_a * log_sig
    a = jnp.exp(log_a)
    om = 1.0 - a * a
    u = om * lax.rsqrt(jnp.maximum(om, 1e-30)) * (gate_x * xc)

    G = T // SUBLANES
    A = a.reshape(G, SUBLANES, W)
    H = u.reshape(G, SUBLANES, W)
    sub = lax.broadcasted_iota(jnp.int32, (G, SUBLANES, W), 1)
    sh = 1
    while sh < SUBLANES:
        keep = sub >= sh
        A_s = jnp.where(keep, pltpu.roll(A, sh, axis=1), 1.0)
        H_s = jnp.where(keep, pltpu.roll(H, sh, axis=1), 0.0)
        H = A * H_s + H
        A = A * A_s
        sh *= 2
    gy = jax.nn.gelu(y_ref[...].astype(F32))
    carry = hcar[...]
    hs = []
    for g in range(G):
        hs.append(H[g] + A[g] * carry)
        carry = H[g, SUBLANES - 1:SUBLANES, :] + A[g, SUBLANES - 1:SUBLANES, :] * carry
        if g % 2 == 1:
            rows = slice((g - 1) * SUBLANES, (g + 1) * SUBLANES)
            h = jnp.concatenate(hs, axis=0)
            o_ref[rows, :] = (h * gy[rows, :]).astype(o_ref.dtype)
            hs = []
    hcar[...] = carry


def _lru_call(proj, cw, cb, wa, ba, wx, bx, ll, S, W, tm):
    N = proj.shape[0]
    tpb = S // tm
    blk = W // LRU_BLOCKS
    vec = lambda: pl.BlockSpec((1, W), lambda i: (0, 0))
    return pl.pallas_call(
        functools.partial(_lru_kernel, tpb=tpb),
        out_shape=jax.ShapeDtypeStruct((N, W), BF16),
        grid=(N // tm,),
        in_specs=[pl.BlockSpec((tm, W), lambda i: (i, 0)),
                  pl.BlockSpec((tm, W), lambda i: (i, 1)),
                  pl.BlockSpec((CONV_WIDTH, W), lambda i: (0, 0)),
                  vec(),
                  pl.BlockSpec((LRU_BLOCKS, blk, blk), lambda i: (0, 0, 0)),
                  vec(),
                  pl.BlockSpec((LRU_BLOCKS, blk, blk), lambda i: (0, 0, 0)),
                  vec(),
                  vec()],
        out_specs=pl.BlockSpec((tm, W), lambda i: (i, 0)),
        scratch_shapes=[pltpu.VMEM((tm + SUBLANES, W), F32), pltpu.VMEM((1, W), F32)],
        compiler_params=_cparams(("arbitrary",)),
        name="lru",
    )(proj, proj, cw, cb, wa, ba, wx, bx, ll)


def _pool_kernel(x_ref, pw_ref, ps_ref, o_ref, xbuf, *, tpb):
    T, W = x_ref.shape
    G = len(POOL_WINDOWS)
    gw = W // G
    i = pl.program_id(0)

    @pl.when(i % tpb == 0)
    def _():
        xbuf[0:POOL_HALO, :] = jnp.zeros((POOL_HALO, W), F32)

    xbuf[POOL_HALO:POOL_HALO + T, :] = x_ref[...].astype(F32)
    t = (i % tpb) * T + lax.broadcasted_iota(jnp.int32, (T, 1), 0)
    outs = []
    for g, w in enumerate(POOL_WINDOWS):
        cols = slice(g * gw, (g + 1) * gw)
        xg = xbuf[POOL_HALO:POOL_HALO + T, cols]
        ws = xg
        for d in range(1, w):
            ws = ws + xbuf[POOL_HALO - d:POOL_HALO - d + T, cols]
        cnt = jnp.minimum(t + 1, w).astype(F32)
        pooled = (ws / cnt - xg).astype(BF16)
        outs.append(_dot(pooled, pw_ref[g]))
    xbuf[0:POOL_HALO, :] = xbuf[T:T + POOL_HALO, :]
    o_ref[...] = (jnp.concatenate(outs, axis=1) * ps_ref[...]).astype(o_ref.dtype)


def _pool_call(proj, pw, ps, S, W, tm):
    N = proj.shape[0]
    tpb = S // tm
    G = len(POOL_WINDOWS)
    return pl.pallas_call(
        functools.partial(_pool_kernel, tpb=tpb),
        out_shape=jax.ShapeDtypeStruct((N, W), BF16),
        grid=(N // tm,),
        in_specs=[pl.BlockSpec((tm, W), lambda i: (i, 2)),
                  pl.BlockSpec((G, W // G, W // G), lambda i: (0, 0, 0)),
                  pl.BlockSpec((1, W), lambda i: (0, 0))],
        out_specs=pl.BlockSpec((tm, W), lambda i: (i, 0)),
        scratch_shapes=[pltpu.VMEM((tm + POOL_HALO, W), F32)],
        compiler_params=_cparams(("arbitrary",)),
        name="pool",
    )(proj, pw, ps)


def _rope_tables(S):
    hd = ATT_HEAD_DIM
    pos = jnp.arange(S, dtype=F32)
    inv = ROPE_THETA ** (-jnp.arange(0, hd, 2, dtype=F32) / hd)
    ang = pos[:, None] * inv[None, :]
    c, s = jnp.cos(ang), jnp.sin(ang)
    cs = jnp.concatenate([c, c, c, c], axis=1)
    sn = jnp.concatenate([-s, s, -s, s], axis=1)
    return cs, sn


def _attn_kernel(safe_ref, q_ref, k_ref, v_ref, lp_ref, sg_ref, o_ref, vt_s, acct_s, m_s, l_s, acc_s, *,
                 tk, lam_init):
    tq = q_ref.shape[0]
    S = k_ref.shape[0]
    i = pl.program_id(2)
    lane = lax.broadcasted_iota(jnp.int32, (tq, LANES), 1)
    q = q_ref[...]
    zero = jnp.zeros_like(q)
    qc = (jnp.where(lane < ATT_HEAD_DIM, q, zero), jnp.where(lane < ATT_HEAD_DIM, zero, q))
    safe = safe_ref[0] == 1
    n_full = (i * tq) // tk
    n_all = ((i + 1) * tq + tk - 1) // tk

    def scores(c, j, k, masked):
        s = _dot_nt(qc[c], k)
        if masked:
            qpos = i * tq + lax.broadcasted_iota(jnp.int32, (tq, tk), 0)
            kpos = j * tk + lax.broadcasted_iota(jnp.int32, (tq, tk), 1)
            s = jnp.where(kpos <= qpos, s, -jnp.inf)
        return s

    def loops(tile):
        lax.fori_loop(0, n_full, lambda j, c: tile(j, False) or c, 0)
        lax.fori_loop(n_full, n_all, lambda j, c: tile(j, True) or c, 0)

    def finalize(a1, l1, a2, l2):
        lp = lp_ref[...]
        lam = (jnp.exp(jnp.sum(lp[0:1] * lp[1:2], axis=-1, keepdims=True))
               - jnp.exp(jnp.sum(lp[2:3] * lp[3:4], axis=-1, keepdims=True)) + lam_init)
        o = a1 / l1 - lam * (a2 / l2)
        o = o * lax.rsqrt(jnp.mean(o * o, axis=-1, keepdims=True) + EPS) * sg_ref[...]
        o_ref[...] = (o * (1.0 - lam_init)).astype(o_ref.dtype)

    @pl.when(jnp.logical_and(safe, i == 0))
    def _():
        tail = jnp.where(lax.broadcasted_iota(jnp.int32, (ATT_VT_ROWS - LANES, tk), 0) == 0,
                         1.0, 0.0).astype(BF16)

        def fill(j, c):
            off = pl.multiple_of(j * tk, tk)
            vt_s[0:LANES, pl.ds(off, tk)] = v_ref[pl.ds(off, tk), :].astype(F32).T.astype(BF16)
            vt_s[LANES:ATT_VT_ROWS, pl.ds(off, tk)] = tail
            return c

        lax.fori_loop(0, S // tk, fill, 0)

    @pl.when(safe)
    def _():
        acct_s[...] = jnp.zeros_like(acct_s)

        def tile(j, masked):
            off = pl.multiple_of(j * tk, tk)
            k = k_ref[pl.ds(off, tk), :]
            vt = vt_s[:, pl.ds(off, tk)]
            if masked and tq == tk:
                hq = tq // 2
                tri = (lax.broadcasted_iota(jnp.int32, (hq, hq), 0)
                       <= lax.broadcasted_iota(jnp.int32, (hq, hq), 1))
                for c in range(2):
                    s_a = jnp.where(tri, _dot_nt(k[0:hq], qc[c][0:hq]), -jnp.inf)
                    acct_s[c, :, 0:hq] += _dot(vt[:, 0:hq], jnp.exp2(s_a).astype(BF16))
                    s_b = _dot_nt(k, qc[c][hq:tq])
                    s_b = jnp.concatenate([s_b[0:hq], jnp.where(tri, s_b[hq:tk], -jnp.inf)], axis=0)
                    acct_s[c, :, hq:tq] += _dot(vt, jnp.exp2(s_b).astype(BF16))
                return
            for c in range(2):
                s = _dot_nt(k, qc[c])
                if masked:
                    kpos = j * tk + lax.broadcasted_iota(jnp.int32, (tk, tq), 0)
                    qpos = i * tq + lax.broadcasted_iota(jnp.int32, (tk, tq), 1)
                    s = jnp.where(kpos <= qpos, s, -jnp.inf)
                acct_s[c] += _dot(vt, jnp.exp2(s).astype(BF16))

        loops(tile)
        a1, a2 = acct_s[0], acct_s[1]
        lp = lp_ref[...]
        lam = (jnp.exp(jnp.sum(lp[0:1] * lp[1:2], axis=-1, keepdims=True))
               - jnp.exp(jnp.sum(lp[2:3] * lp[3:4], axis=-1, keepdims=True)) + lam_init)
        o = a1[0:LANES] / a1[LANES:LANES + 1] - lam * (a2[0:LANES] / a2[LANES:LANES + 1])
        o = o * lax.rsqrt(jnp.mean(o * o, axis=0, keepdims=True) + EPS)
        o_ref[...] = (o.T * sg_ref[...] * (1.0 - lam_init)).astype(o_ref.dtype)

    @pl.when(jnp.logical_not(safe))
    def _():
        m_s[...] = jnp.full_like(m_s, -jnp.inf)
        l_s[...] = jnp.zeros_like(l_s)
        acc_s[...] = jnp.zeros_like(acc_s)

        def tile(j, masked):
            off = pl.multiple_of(j * tk, tk)
            k = k_ref[pl.ds(off, tk), :]
            v = v_ref[pl.ds(off, tk), :]
            for c in range(2):
                s = scores(c, j, k, masked)
                m_old = m_s[c]
                m_new = jnp.maximum(m_old, jnp.max(s, axis=-1, keepdims=True))
                alpha = jnp.exp2(m_old - m_new)
                p = jnp.exp2(s - m_new)
                l_s[c] = alpha * l_s[c] + jnp.sum(p, axis=-1, keepdims=True)
                acc_s[c] = alpha * acc_s[c] + _dot(p.astype(BF16), v)
                m_s[c] = m_new

        loops(tile)
        finalize(acc_s[0], l_s[0], acc_s[1], l_s[1])


def _attn_call(qkv, safe, lamp, sg, B, S, lam_init, tq, tk):
    N = qkv.shape[0]
    W = qkv.shape[1] // 3
    H = W // LANES
    nq = S // tq
    grid_spec = pltpu.PrefetchScalarGridSpec(
        num_scalar_prefetch=1,
        grid=(B, H, nq),
        in_specs=[pl.BlockSpec((tq, LANES), lambda b, h, i, f: (b * nq + i, h)),
                  pl.BlockSpec((S, LANES), lambda b, h, i, f: (b, H + h)),
                  pl.BlockSpec((S, LANES), lambda b, h, i, f: (b, 2 * H + h)),
                  pl.BlockSpec((SUBLANES, LANES), lambda b, h, i, f: (0, 0)),
                  pl.BlockSpec((1, LANES), lambda b, h, i, f: (0, 0))],
        out_specs=pl.BlockSpec((tq, LANES), lambda b, h, i, f: (b * nq + i, h)),
        scratch_shapes=[pltpu.VMEM((ATT_VT_ROWS, S), BF16), pltpu.VMEM((2, ATT_VT_ROWS, tq), F32),
                        pltpu.VMEM((2, tq, 1), F32), pltpu.VMEM((2, tq, 1), F32),
                        pltpu.VMEM((2, tq, LANES), F32)])
    return pl.pallas_call(
        functools.partial(_attn_kernel, tk=tk, lam_init=lam_init),
        out_shape=jax.ShapeDtypeStruct((N, W), BF16),
        grid_spec=grid_spec,
        compiler_params=_cparams(("arbitrary", "arbitrary", "arbitrary")),
        name="attn",
    )(safe, qkv, qkv, qkv, lamp, sg)


def _split_bf16(x):
    hi = x.astype(BF16)
    lo = (x - hi.astype(F32)).astype(BF16)
    return hi, lo


def _route(sel, aff):
    epg = EXPERTS_PER_GROUP
    T = sel.shape[1]
    rows = [sel[e:e + 1, :] for e in range(N_EXPERTS)]
    arow = [aff[e:e + 1, :] for e in range(N_EXPERTS)]
    best = None
    gidx = None
    for g in range(N_GROUPS):
        r = rows[g * epg:(g + 1) * epg]
        gs = None
        for a in range(epg):
            for b in range(a + 1, epg):
                pair = r[a] + r[b]
                gs = pair if gs is None else jnp.maximum(gs, pair)
        if best is None:
            best, gidx = gs, jnp.zeros((1, T), jnp.int32)
        else:
            better = gs > best
            gidx = jnp.where(better, g, gidx)
            best = jnp.where(better, gs, best)
    ig, ag = [], []
    for a in range(epg):
        s_a, f_a = rows[a], arow[a]
        for g in range(1, N_GROUPS):
            pick = gidx == g
            s_a = jnp.where(pick, rows[g * epg + a], s_a)
            f_a = jnp.where(pick, arow[g * epg + a], f_a)
        ig.append(s_a)
        ag.append(f_a)
    v1, a1, w1 = ig[0], jnp.zeros((1, T), jnp.int32), ag[0]
    for a in range(1, epg):
        better = ig[a] > v1
        a1 = jnp.where(better, a, a1)
        w1 = jnp.where(better, ag[a], w1)
        v1 = jnp.where(better, ig[a], v1)
    v2 = jnp.full((1, T), -jnp.inf, F32)
    a2 = jnp.zeros((1, T), jnp.int32)
    w2 = jnp.zeros((1, T), F32)
    for a in range(epg):
        better = jnp.logical_and(a1 != a, ig[a] > v2)
        a2 = jnp.where(better, a, a2)
        w2 = jnp.where(better, ag[a], w2)
        v2 = jnp.where(better, ig[a], v2)
    tot = w1 + w2
    e1 = gidx * epg + a1
    e2 = gidx * epg + a2
    erow = lax.broadcasted_iota(jnp.int32, (N_EXPERTS, T), 0)
    comb = jnp.where(erow == e1, w1 / tot, 0.0) + jnp.where(erow == e2, w2 / tot, 0.0)
    return comb, gidx


def _merge_kernel(x_ref, bl_ref, bp_ref, ba_ref, gt_ref, mod_ref, wb_ref, wo_ref, g2_ref, rw_ref,
                  rb_ref, xo_ref, h2_ref, cb_ref, gr_ref):
    T, D = x_ref.shape
    m = mod_ref[0]
    merged = None
    for n, br in enumerate((bl_ref, bp_ref, ba_ref)):
        gate = jax.nn.sigmoid(gt_ref[:, n * D:(n + 1) * D].astype(F32))
        term = gate * _dot(br[...], wb_ref[n])
        merged = term if merged is None else merged + term
    y = _dot(merged.astype(BF16), wo_ref[...])
    x = x_ref[...] + m[2:3] * y
    xo_ref[...] = x
    var = jnp.mean(x * x, axis=-1, keepdims=True)
    h2 = x * lax.rsqrt(var + EPS) * g2_ref[...] * (1.0 + m[4:5]) + m[3:4]
    h2_ref[...] = h2.astype(BF16)
    h_hi, h_lo = _split_bf16(h2)
    rw = rw_ref[...]
    w_hi, w_lo = _split_bf16(rw)
    logits = _dot(h_hi, w_hi) + _dot(h_hi, w_lo) + _dot(h_lo, w_hi)
    lt = logits.T[0:N_EXPERTS, :]
    aff = jax.nn.sigmoid(lt)
    comb, gidx = _route(aff + rb_ref[...], aff)
    comb = jnp.concatenate([comb, jnp.zeros((LANES - N_EXPERTS, T), F32)], axis=0)
    cb_ref[...] = comb.T
    gr_ref[...] = jnp.broadcast_to(gidx.astype(F32), (SUBLANES, T))


def _merge_call(x, br_l, br_p, br_a, gates, mod, wb, wo, g2, rw, rb, S, tm):
    N, D = x.shape
    tpb = S // tm
    tok = lambda: pl.BlockSpec((tm, D), lambda i: (i, 0))
    return pl.pallas_call(
        _merge_kernel,
        out_shape=(jax.ShapeDtypeStruct((N, D), F32), jax.ShapeDtypeStruct((N, D), BF16),
                   jax.ShapeDtypeStruct((N, LANES), F32), jax.ShapeDtypeStruct((SUBLANES, N), F32)),
        grid=(N // tm,),
        in_specs=[tok(), tok(), tok(), tok(),
                  pl.BlockSpec((tm, N_BRANCH * D), lambda i: (i, 0)),
                  pl.BlockSpec((1, SUBLANES, D), lambda i: (i // tpb, 0, 0)),
                  pl.BlockSpec((N_BRANCH, D, D), lambda i: (0, 0, 0)),
                  pl.BlockSpec((D, D), lambda i: (0, 0)),
                  pl.BlockSpec((1, D), lambda i: (0, 0)),
                  pl.BlockSpec((D, LANES), lambda i: (0, 0)),
                  pl.BlockSpec((N_EXPERTS, 1), lambda i: (0, 0))],
        out_specs=(tok(), tok(), pl.BlockSpec((tm, LANES), lambda i: (i, 0)),
                   pl.BlockSpec((SUBLANES, tm), lambda i: (0, i))),
        compiler_params=_cparams(("arbitrary",)),
        name="merge",
    )(x, br_l, br_p, br_a, gates, mod, wb, wo, g2, rw, rb)


def _moe_kernel(h_ref, cb_ref, gr_ref, x_ref, mod_ref, wg_ref, wu_ref, wd_ref, o_ref,
                dm_s, xs_s, cw_s, ys_s, seg_s):
    T, D = h_ref.shape
    R = MOE_CHUNK_ROWS
    e = pl.program_id(1)

    @pl.when(e == 0)
    def _():
        gid = gr_ref[0:1, :]
        grow = lax.broadcasted_iota(jnp.int32, (2 * SUBLANES, T), 0).astype(F32)
        oh = grow == gid
        ohf = jnp.where(oh, 1.0, 0.0)
        ohb = ohf.astype(BF16)
        rb = MOE_RANK_BLOCK
        tri = jnp.where(lax.broadcasted_iota(jnp.int32, (rb, rb), 0)
                        < lax.broadcasted_iota(jnp.int32, (rb, rb), 1), 1.0, 0.0).astype(BF16)
        count = jnp.zeros((2 * SUBLANES, 1), F32)
        ranks = []
        for b in range(T // rb):
            cols = slice(b * rb, (b + 1) * rb)
            ranks.append(_dot(ohb[:, cols], tri) + count)
            count = count + jnp.sum(ohf[:, cols], axis=1, keepdims=True)
        rank = jnp.concatenate(ranks, axis=1)
        rowi = lax.broadcasted_iota(jnp.int32, (2 * SUBLANES, 1), 0)
        start = jnp.zeros((2 * SUBLANES, 1), F32)
        run = jnp.zeros((1, 1), F32)
        for g in range(N_GROUPS):
            start = jnp.where(rowi == g, run, start)
            seg_s[g] = run[0, 0].astype(jnp.int32)
            run = run + count[g:g + 1, :]
        seg_s[N_GROUPS] = run[0, 0].astype(jnp.int32)
        pos = jnp.sum(jnp.where(oh, start + rank, 0.0), axis=0, keepdims=True).astype(jnp.int32)
        dm = jnp.where(lax.broadcasted_iota(jnp.int32, (T, T), 0) == pos, 1.0, 0.0).astype(BF16)
        dm_s[...] = dm
        xs_s[...] = _dot(dm, h_ref[...]).astype(BF16)
        c_hi, c_lo = _split_bf16(cb_ref[...])
        cw_s[...] = _dot(dm, c_hi) + _dot(dm, c_lo)
        ys_s[...] = jnp.zeros_like(ys_s)

    g = (e * MOE_EXPERTS_PER_STEP) // EXPERTS_PER_GROUP
    lo = seg_s[g]
    hi = seg_s[g + 1]
    lane = lax.broadcasted_iota(jnp.int32, (R, LANES), 1)

    def chunk(c, carry):
        rows = pl.ds(pl.multiple_of(c * R, R), R)
        xc = xs_s[rows, :]
        cws = cw_s[rows, :]
        y = None
        for s in range(MOE_EXPERTS_PER_STEP):
            gg = _dot(xc, wg_ref[s])
            uu = _dot(xc, wu_ref[s])
            cw = jnp.sum(jnp.where(lane == e * MOE_EXPERTS_PER_STEP + s, cws, 0.0), axis=-1, keepdims=True)
            act = (gg * jax.nn.sigmoid(gg)) * uu * cw
            part = _dot(act.astype(BF16), wd_ref[s])
            y = part if y is None else y + part
        ys_s[rows, :] += y
        return carry

    lax.fori_loop(lo // R, (hi + R - 1) // R, chunk, 0)

    @pl.when(e == pl.num_programs(1) - 1)
    def _():
        y = lax.dot_general(dm_s[...], ys_s[...].astype(BF16), (((0,), (0,)), ((), ())),
                            preferred_element_type=F32)
        o_ref[...] = x_ref[...] + mod_ref[0][5:6] * y


def _moe_call(h2, comb, grp, x, mod, wg, wu, wd, S, tm):
    N, D = x.shape
    E, _, F = wg.shape
    tpb = S // tm
    return pl.pallas_call(
        _moe_kernel,
        out_shape=jax.ShapeDtypeStruct((N, D), F32),
        grid=(N // tm, E // MOE_EXPERTS_PER_STEP),
        in_specs=[pl.BlockSpec((tm, D), lambda i, e: (i, 0)),
                  pl.BlockSpec((tm, LANES), lambda i, e: (i, 0)),
                  pl.BlockSpec((SUBLANES, tm), lambda i, e: (0, i)),
                  pl.BlockSpec((tm, D), lambda i, e: (i, 0)),
                  pl.BlockSpec((1, SUBLANES, D), lambda i, e: (i // tpb, 0, 0)),
                  pl.BlockSpec((MOE_EXPERTS_PER_STEP, D, F), lambda i, e: (e, 0, 0)),
                  pl.BlockSpec((MOE_EXPERTS_PER_STEP, D, F), lambda i, e: (e, 0, 0)),
                  pl.BlockSpec((MOE_EXPERTS_PER_STEP, F, D), lambda i, e: (e, 0, 0))],
        out_specs=pl.BlockSpec((tm, D), lambda i, e: (i, 0)),
        scratch_shapes=[pltpu.VMEM((tm, tm), BF16), pltpu.VMEM((tm, D), BF16),
                        pltpu.VMEM((tm, LANES), F32), pltpu.VMEM((tm, D), F32),
                        pltpu.SMEM((SUBLANES,), jnp.int32)],
        compiler_params=_cparams(("arbitrary", "arbitrary")),
        name="moe",
    )(h2, comb, grp, x, mod, wg, wu, wd)


def _tiles(S):
    t = lambda want: min(want, S)
    return dict(inproj=t(512), lru=t(256), pool=t(256), qk=t(512), tq=t(1024), tk=t(1024),
                merge=t(512), moe=t(1024))


def kernel(x, c, w_ada, b_ada, norm1_g, norm2_g, w_in, conv_w, conv_b, lru_wa, lru_ba, lru_wx, lru_bx,
           lru_l, pool_w, pool_scale, qn_g, kn_g, lam_q1, lam_k1, lam_q2, lam_k2, subln_g, w_branch,
           w_out, router_w, router_b, w_gate, w_up, w_down):
    B, S, D = x.shape
    L = w_ada.shape[0]
    N = B * S
    W = D
    ts = _tiles(S)
    mod = _mod_call(c, w_ada, b_ada)
    cs, sn = _rope_tables(S)
    rw = jnp.pad(router_w, ((0, 0), (0, LANES - N_EXPERTS)))
    rb = router_b.reshape(N_EXPERTS, 1)
    hd = ATT_HEAD_DIM
    xf = x.reshape(N, D)
    for l in range(L):
        lam_init = 0.8 - 0.6 * math.exp(-0.3 * l)
        w_l = w_in[l].astype(BF16)
        g1 = norm1_g[l].reshape(1, D)
        qg = jnp.tile(qn_g[l], 2).reshape(1, LANES)
        kg = jnp.tile(kn_g[l], 2).reshape(1, LANES)
        proj = _inproj_call(xf, mod[l], g1, w_l[:, 0:3 * D], S, ts["inproj"])
        qkv = _inproj_call(xf, mod[l], g1, w_l[:, 3 * D:6 * D], S, ts["inproj"], rope=(cs, sn, qg, kg))
        gates = _inproj_call(xf, mod[l], g1, w_l[:, 6 * D:9 * D], S, ts["inproj"])
        br_l = _lru_call(proj, conv_w[l], conv_b[l].reshape(1, W), lru_wa[l].astype(BF16),
                         lru_ba[l].reshape(1, W), lru_wx[l].astype(BF16), lru_bx[l].reshape(1, W),
                         lru_l[l].reshape(1, W), S, W, ts["lru"])
        br_p = _pool_call(proj, pool_w[l].astype(BF16), pool_scale[l].reshape(1, W), S, W, ts["pool"])
        lamp = jnp.zeros((SUBLANES, LANES), F32)
        lamp = lamp.at[0, :hd].set(lam_q1[l]).at[1, :hd].set(lam_k1[l])
        lamp = lamp.at[2, :hd].set(lam_q2[l]).at[3, :hd].set(lam_k2[l])
        bound = math.sqrt(hd) * jnp.max(jnp.abs(qn_g[l])) * jnp.max(jnp.abs(kn_g[l]))
        safe = (bound < SAFE_SCORE_BOUND).astype(jnp.int32).reshape(1)
        br_a = _attn_call(qkv, safe, lamp, subln_g[l].reshape(1, LANES), B, S, lam_init,
                          ts["tq"], ts["tk"])
        xf, h2, comb, grp = _merge_call(xf, br_l, br_p, br_a, gates, mod[l], w_branch[l].astype(BF16),
                                        w_out[l].astype(BF16), norm2_g[l].reshape(1, D), rw, rb, S,
                                        ts["merge"])
        xf = _moe_call(h2, comb, grp, xf, mod[l], w_gate[l].astype(BF16), w_up[l].astype(BF16),
                       w_down[l].astype(BF16), S, ts["moe"])
    return xf.reshape(B, S, D)
```

```python
import functools
import math

import jax
import jax.numpy as jnp
from jax import lax
from jax.experimental import pallas as pl
from jax.experimental.pallas import tpu as pltpu

F32 = jnp.float32
BF16 = jnp.bfloat16

EPS = 1e-6
LANES = 128
SUBLANES = 8
LRU_BLOCKS = 8
CONV_WIDTH = 4
LRU_C = 8.0
POOL_WINDOWS = (2, 4, 8, 16)
POOL_HALO = 16
ATT_HEADS = 8
ATT_HEAD_DIM = 64
ATT_VT_ROWS = 144
ROPE_THETA = 10000.0
N_BRANCH = 3
N_EXPERTS = 16
N_GROUPS = 4
EXPERTS_PER_GROUP = N_EXPERTS // N_GROUPS
MERGE_ROW_SPLITS = 2
MOE_CHUNK_ROWS = 128
MOE_EXPERTS_PER_STEP = 2
MOE_RANK_BLOCK = 256
VMEM_LIMIT = 56 * 1024 * 1024
LOG2E = math.log2(math.e)
SAFE_SCORE_BOUND = 32.0


def _cparams(sem):
    return pltpu.CompilerParams(dimension_semantics=sem, vmem_limit_bytes=VMEM_LIMIT)


def _dot(a, b):
    return jnp.dot(a, b, preferred_element_type=F32)


def _dot_nt(a, b):
    return lax.dot_general(a, b, (((1,), (1,)), ((), ())), preferred_element_type=F32)


def _mod_kernel(c_ref, w_ref, b_ref, o_ref):
    c = c_ref[...]
    ca = c * jax.nn.sigmoid(c)
    o_ref[0] = jnp.dot(ca, w_ref[0], preferred_element_type=F32,
                       precision=lax.Precision.HIGHEST) + b_ref[0]


def _mod_call(c, w_ada, b_ada):
    L, D, D6 = w_ada.shape
    B = c.shape[0]
    cp = jnp.zeros((SUBLANES, D), F32).at[:B].set(c)
    out = pl.pallas_call(
        _mod_kernel,
        out_shape=jax.ShapeDtypeStruct((L, SUBLANES, D6), F32),
        grid=(L, D6 // D),
        in_specs=[pl.BlockSpec((SUBLANES, D), lambda l, j: (0, 0)),
                  pl.BlockSpec((1, D, D), lambda l, j: (l, 0, j)),
                  pl.BlockSpec((1, 1, D), lambda l, j: (l, 0, j))],
        out_specs=pl.BlockSpec((1, SUBLANES, D), lambda l, j: (l, 0, j)),
        compiler_params=_cparams(("arbitrary", "arbitrary")),
        name="mod",
    )(cp, w_ada, b_ada.reshape(L, 1, D6))
    mod = out[:, :B].reshape(L, B, 6, D)
    return jnp.pad(mod, ((0, 0), (0, 0), (0, 2), (0, 0)))


def _qk_norm_rope(x, g, cs, sn, first, lower, scale):
    hd = ATT_HEAD_DIM
    sq = x * x
    s1 = jnp.sum(jnp.where(first, sq, 0.0), axis=-1, keepdims=True)
    s2 = jnp.sum(jnp.where(first, 0.0, sq), axis=-1, keepdims=True)
    r = jnp.where(first, lax.rsqrt(s1 * (1.0 / hd) + EPS), lax.rsqrt(s2 * (1.0 / hd) + EPS))
    xn = x * r * g
    partner = jnp.where(lower, pltpu.roll(xn, LANES - hd // 2, axis=1), pltpu.roll(xn, hd // 2, axis=1))
    return (xn * cs + partner * sn) * scale


def _inproj_kernel(x_ref, mod_ref, g_ref, w_ref, *rest, qk_cols):
    o_ref = rest[-1]
    x = x_ref[...]
    m = mod_ref[0]
    var = jnp.mean(x * x, axis=-1, keepdims=True)
    y = x * lax.rsqrt(var + EPS) * g_ref[...]
    h = y * (1.0 + m[1:2]) + m[0:1]
    res = _dot(h.astype(BF16), w_ref[...])
    if not qk_cols:
        o_ref[...] = res.astype(BF16)
        return
    cs_ref, sn_ref, qg_ref, kg_ref = rest[:4]
    T = x.shape[0]
    lane = lax.broadcasted_iota(jnp.int32, (T, LANES), 1)
    first = lane < ATT_HEAD_DIM
    lower = (lane % ATT_HEAD_DIM) < (ATT_HEAD_DIM // 2)
    cs, sn = cs_ref[...], sn_ref[...]
    half = qk_cols // 2
    for c in range(qk_cols // LANES):
        cols = slice(c * LANES, (c + 1) * LANES)
        is_q = c * LANES < half
        g = qg_ref[...] if is_q else kg_ref[...]
        scale = ATT_HEAD_DIM ** -0.5 * LOG2E if is_q else 1.0
        o_ref[:, cols] = _qk_norm_rope(res[:, cols], g, cs, sn, first, lower, scale).astype(BF16)
    o_ref[:, qk_cols:] = res[:, qk_cols:].astype(BF16)


def _inproj_call(x, mod, g, w, S, tm, rope=None):
    N, D = x.shape
    W = w.shape[1]
    tpb = S // tm
    in_specs = [pl.BlockSpec((tm, D), lambda i: (i, 0)),
                pl.BlockSpec((1, SUBLANES, D), lambda i: (i // tpb, 0, 0)),
                pl.BlockSpec((1, D), lambda i: (0, 0)),
                pl.BlockSpec((D, W), lambda i: (0, 0))]
    args = [x, mod, g, w]
    if rope is not None:
        in_specs += [pl.BlockSpec((tm, LANES), lambda i: (i % tpb, 0)),
                     pl.BlockSpec((tm, LANES), lambda i: (i % tpb, 0)),
                     pl.BlockSpec((1, LANES), lambda i: (0, 0)),
                     pl.BlockSpec((1, LANES), lambda i: (0, 0))]
        args += list(rope)
    return pl.pallas_call(
        functools.partial(_inproj_kernel, qk_cols=0 if rope is None else 2 * W // 3),
        out_shape=jax.ShapeDtypeStruct((N, W), BF16),
        grid=(N // tm,),
        in_specs=in_specs,
        out_specs=pl.BlockSpec((tm, W), lambda i: (i, 0)),
        compiler_params=_cparams(("arbitrary",)),
        name="inproj_qkv" if rope is not None else "inproj",
    )(*args)


def _lru_kernel(x_ref, y_ref, cw_ref, cb_ref, wa_ref, ba_ref, wx_ref, bx_ref, ll_ref, o_ref,
                xbuf, hcar, *, tpb):
    T, W = x_ref.shape
    blk = W // LRU_BLOCKS
    i = pl.program_id(0)

    @pl.when(i % tpb == 0)
    def _():
        xbuf[0:SUBLANES, :] = jnp.zeros((SUBLANES, W), F32)
        hcar[...] = jnp.zeros_like(hcar)

    xbuf[SUBLANES:SUBLANES + T, :] = x_ref[...].astype(F32)
    xc = cb_ref[...] + xbuf[SUBLANES:SUBLANES + T, :] * cw_ref[CONV_WIDTH - 1:CONV_WIDTH, :]
    for j in range(CONV_WIDTH - 1):
        d = CONV_WIDTH - 1 - j
        xc = xc + xbuf[SUBLANES - d:SUBLANES - d + T, :] * cw_ref[j:j + 1, :]
    xbuf[0:SUBLANES, :] = xbuf[T:T + SUBLANES, :]

    xcb = xc.astype(BF16)
    ga, gx = [], []
    for n in range(LRU_BLOCKS):
        xs = xcb[:, n * blk:(n + 1) * blk]
        ga.append(_dot(xs, wa_ref[n]))
        gx.append(_dot(xs, wx_ref[n]))
    gate_a = jax.nn.sigmoid(jnp.concatenate(ga, axis=1) + ba_ref[...])
    gate_x = jax.nn.sigmoid(jnp.concatenate(gx, axis=1) + bx_ref[...])
    ll = ll_ref[...]
    log_sig = jnp.minimum(ll, 0.0) - jnp.log1p(jnp.exp(-jnp.abs(ll)))
    log_a = LRU_C * gate_a * log_sig
    a = jnp.exp(log_a)
    om = 1.0 - a * a
    u = om * lax.rsqrt(jnp.maximum(om, 1e-30)) * (gate_x * xc)

    G = T // SUBLANES
    A = a.reshape(G, SUBLANES, W)
    H = u.reshape(G, SUBLANES, W)
    sub = lax.broadcasted_iota(jnp.int32, (G, SUBLANES, W), 1)
    sh = 1
    while sh < SUBLANES:
        keep = sub >= sh
        A_s = jnp.where(keep, pltpu.roll(A, sh, axis=1), 1.0)
        H_s = jnp.where(keep, pltpu.roll(H, sh, axis=1), 0.0)
        H = A * H_s + H
        A = A * A_s
        sh *= 2
    gy = jax.nn.gelu(y_ref[...].astype(F32))
    carry = hcar[...]
    hs = []
    for g in range(G):
        hs.append(H[g] + A[g] * carry)
        carry = H[g, SUBLANES - 1:SUBLANES, :] + A[g, SUBLANES - 1:SUBLANES, :] * carry
        if g % 2 == 1:
            rows = slice((g - 1) * SUBLANES, (g + 1) * SUBLANES)
            h = jnp.concatenate(hs, axis=0)
            o_ref[rows, :] = (h * gy[rows, :]).astype(o_ref.dtype)
            hs = []
    hcar[...] = carry


def _lru_call(proj, cw, cb, wa, ba, wx, bx, ll, S, W, tm):
    N = proj.shape[0]
    tpb = S // tm
    blk = W // LRU_BLOCKS
    vec = lambda: pl.BlockSpec((1, W), lambda i: (0, 0))
    return pl.pallas_call(
        functools.partial(_lru_kernel, tpb=tpb),
        out_shape=jax.ShapeDtypeStruct((N, W), BF16),
        grid=(N // tm,),
        in_specs=[pl.BlockSpec((tm, W), lambda i: (i, 0)),
                  pl.BlockSpec((tm, W), lambda i: (i, 1)),
                  pl.BlockSpec((CONV_WIDTH, W), lambda i: (0, 0)),
                  vec(),
                  pl.BlockSpec((LRU_BLOCKS, blk, blk), lambda i: (0, 0, 0)),
                  vec(),
                  pl.BlockSpec((LRU_BLOCKS, blk, blk), lambda i: (0, 0, 0)),
                  vec(),
                  vec()],
        out_specs=pl.BlockSpec((tm, W), lambda i: (i, 0)),
        scratch_shapes=[pltpu.VMEM((tm + SUBLANES, W), F32), pltpu.VMEM((1, W), F32)],
        compiler_params=_cparams(("arbitrary",)),
        name="lru",
    )(proj, proj, cw, cb, wa, ba, wx, bx, ll)


def _pool_kernel(x_ref, pw_ref, ps_ref, o_ref, xbuf, *, tpb):
    T, W = x_ref.shape
    G = len(POOL_WINDOWS)
    gw = W // G
    i = pl.program_id(0)

    @pl.when(i % tpb == 0)
    def _():
        xbuf[0:POOL_HALO, :] = jnp.zeros((POOL_HALO, W), F32)

    xbuf[POOL_HALO:POOL_HALO + T, :] = x_ref[...].astype(F32)
    t = (i % tpb) * T + lax.broadcasted_iota(jnp.int32, (T, 1), 0)
    outs = []
    for g, w in enumerate(POOL_WINDOWS):
        cols = slice(g * gw, (g + 1) * gw)
        xg = xbuf[POOL_HALO:POOL_HALO + T, cols]
        ws = xg
        for d in range(1, w):
            ws = ws + xbuf[POOL_HALO - d:POOL_HALO - d + T, cols]
        cnt = jnp.minimum(t + 1, w).astype(F32)
        pooled = (ws / cnt - xg).astype(BF16)
        outs.append(_dot(pooled, pw_ref[g]))
    xbuf[0:POOL_HALO, :] = xbuf[T:T + POOL_HALO, :]
    o_ref[...] = (jnp.concatenate(outs, axis=1) * ps_ref[...]).astype(o_ref.dtype)


def _pool_call(proj, pw, ps, S, W, tm):
    N = proj.shape[0]
    tpb = S // tm
    G = len(POOL_WINDOWS)
    return pl.pallas_call(
        functools.partial(_pool_kernel, tpb=tpb),
        out_shape=jax.ShapeDtypeStruct((N, W), BF16),
        grid=(N // tm,),
        in_specs=[pl.BlockSpec((tm, W), lambda i: (i, 2)),
                  pl.BlockSpec((G, W // G, W // G), lambda i: (0, 0, 0)),
                  pl.BlockSpec((1, W), lambda i: (0, 0))],
        out_specs=pl.BlockSpec((tm, W), lambda i: (i, 0)),
        scratch_shapes=[pltpu.VMEM((tm + POOL_HALO, W), F32)],
        compiler_params=_cparams(("arbitrary",)),
        name="pool",
    )(proj, pw, ps)


def _rope_tables(S):
    hd = ATT_HEAD_DIM
    pos = jnp.arange(S, dtype=F32)
    inv = ROPE_THETA ** (-jnp.arange(0, hd, 2, dtype=F32) / hd)
    ang = pos[:, None] * inv[None, :]
    c, s = jnp.cos(ang), jnp.sin(ang)
    cs = jnp.concatenate([c, c, c, c], axis=1)
    sn = jnp.concatenate([-s, s, -s, s], axis=1)
    return cs, sn


def _attn_kernel(safe_ref, q_ref, k_ref, v_ref, lp_ref, sg_ref, o_ref, vt_s, acct_s, m_s, l_s, acc_s, *,
                 tk, lam_init):
    tq = q_ref.shape[0]
    S = k_ref.shape[0]
    i = pl.program_id(2)
    lane = lax.broadcasted_iota(jnp.int32, (tq, LANES), 1)
    q = q_ref[...]
    zero = jnp.zeros_like(q)
    qc = (jnp.where(lane < ATT_HEAD_DIM, q, zero), jnp.where(lane < ATT_HEAD_DIM, zero, q))
    safe = safe_ref[0] == 1
    n_full = (i * tq) // tk
    n_all = ((i + 1) * tq + tk - 1) // tk

    def scores(c, j, k, masked):
        s = _dot_nt(qc[c], k)
        if masked:
            qpos = i * tq + lax.broadcasted_iota(jnp.int32, (tq, tk), 0)
            kpos = j * tk + lax.broadcasted_iota(jnp.int32, (tq, tk), 1)
            s = jnp.where(kpos <= qpos, s, -jnp.inf)
        return s

    def loops(tile, pair_unroll=False):
        first_single = 0
        if pair_unroll:
            n_pairs = n_full // 2

            def pair(jj, c):
                tile(2 * jj, False)
                tile(2 * jj + 1, False)
                return c

            lax.fori_loop(0, n_pairs, pair, 0)
            first_single = 2 * n_pairs
        lax.fori_loop(first_single, n_full, lambda j, c: tile(j, False) or c, 0)
        lax.fori_loop(n_full, n_all, lambda j, c: tile(j, True) or c, 0)

    def finalize(a1, l1, a2, l2):
        lp = lp_ref[...]
        lam = (jnp.exp(jnp.sum(lp[0:1] * lp[1:2], axis=-1, keepdims=True))
               - jnp.exp(jnp.sum(lp[2:3] * lp[3:4], axis=-1, keepdims=True)) + lam_init)
        o = a1 / l1 - lam * (a2 / l2)
        o = o * lax.rsqrt(jnp.mean(o * o, axis=-1, keepdims=True) + EPS) * sg_ref[...]
        o_ref[...] = (o * (1.0 - lam_init)).astype(o_ref.dtype)

    @pl.when(jnp.logical_and(safe, i == 0))
    def _():
        tail = jnp.where(lax.broadcasted_iota(jnp.int32, (ATT_VT_ROWS - LANES, tk), 0) == 0,
                         1.0, 0.0).astype(BF16)

        def fill(j, c):
            off = pl.multiple_of(j * tk, tk)
            vt_s[0:LANES, pl.ds(off, tk)] = v_ref[pl.ds(off, tk), :].astype(F32).T.astype(BF16)
            vt_s[LANES:ATT_VT_ROWS, pl.ds(off, tk)] = tail
            return c

        lax.fori_loop(0, S // tk, fill, 0)

    @pl.when(safe)
    def _():
        acct_s[...] = jnp.zeros_like(acct_s)

        def tile(j, masked):
            off = pl.multiple_of(j * tk, tk)
            k = k_ref[pl.ds(off, tk), :]
            vt = vt_s[:, pl.ds(off, tk)]
            if masked and tq == tk:
                hq = tq // 2
                tri = (lax.broadcasted_iota(jnp.int32, (hq, hq), 0)
                       <= lax.broadcasted_iota(jnp.int32, (hq, hq), 1))
                for c in range(2):
                    s_a = jnp.where(tri, _dot_nt(k[0:hq], qc[c][0:hq]), -jnp.inf)
                    acct_s[c, :, 0:hq] += _dot(vt[:, 0:hq], jnp.exp2(s_a).astype(BF16))
                    s_b = _dot_nt(k, qc[c][hq:tq])
                    s_b = jnp.concatenate([s_b[0:hq], jnp.where(tri, s_b[hq:tk], -jnp.inf)], axis=0)
                    acct_s[c, :, hq:tq] += _dot(vt, jnp.exp2(s_b).astype(BF16))
                return
            for c in range(2):
                s = _dot_nt(k, qc[c])
                if masked:
                    kpos = j * tk + lax.broadcasted_iota(jnp.int32, (tk, tq), 0)
                    qpos = i * tq + lax.broadcasted_iota(jnp.int32, (tk, tq), 1)
                    s = jnp.where(kpos <= qpos, s, -jnp.inf)
                acct_s[c] += _dot(vt, jnp.exp2(s).astype(BF16))

        loops(tile, pair_unroll=True)
        a1, a2 = acct_s[0], acct_s[1]
        lp = lp_ref[...]
        lam = (jnp.exp(jnp.sum(lp[0:1] * lp[1:2], axis=-1, keepdims=True))
               - jnp.exp(jnp.sum(lp[2:3] * lp[3:4], axis=-1, keepdims=True)) + lam_init)
        o = a1[0:LANES] / a1[LANES:LANES + 1] - lam * (a2[0:LANES] / a2[LANES:LANES + 1])
        o = o * lax.rsqrt(jnp.mean(o * o, axis=0, keepdims=True) + EPS)
        o_ref[...] = (o.T * sg_ref[...] * (1.0 - lam_init)).astype(o_ref.dtype)

    @pl.when(jnp.logical_not(safe))
    def _():
        m_s[...] = jnp.full_like(m_s, -jnp.inf)
        l_s[...] = jnp.zeros_like(l_s)
        acc_s[...] = jnp.zeros_like(acc_s)

        def tile(j, masked):
            off = pl.multiple_of(j * tk, tk)
            k = k_ref[pl.ds(off, tk), :]
            v = v_ref[pl.ds(off, tk), :]
            for c in range(2):
                s = scores(c, j, k, masked)
                m_old = m_s[c]
                m_new = jnp.maximum(m_old, jnp.max(s, axis=-1, keepdims=True))
                alpha = jnp.exp2(m_old - m_new)
                p = jnp.exp2(s - m_new)
                l_s[c] = alpha * l_s[c] + jnp.sum(p, axis=-1, keepdims=True)
                acc_s[c] = alpha * acc_s[c] + _dot(p.astype(BF16), v)
                m_s[c] = m_new

        loops(tile)
        finalize(acc_s[0], l_s[0], acc_s[1], l_s[1])


def _attn_call(qkv, safe, lamp, sg, B, S, lam_init, tq, tk):
    N = qkv.shape[0]
    W = qkv.shape[1] // 3
    H = W // LANES
    nq = S // tq
    grid_spec = pltpu.PrefetchScalarGridSpec(
        num_scalar_prefetch=1,
        grid=(B, H, nq),
        in_specs=[pl.BlockSpec((tq, LANES), lambda b, h, i, f: (b * nq + i, h)),
                  pl.BlockSpec((S, LANES), lambda b, h, i, f: (b, H + h)),
                  pl.BlockSpec((S, LANES), lambda b, h, i, f: (b, 2 * H + h)),
                  pl.BlockSpec((SUBLANES, LANES), lambda b, h, i, f: (0, 0)),
                  pl.BlockSpec((1, LANES), lambda b, h, i, f: (0, 0))],
        out_specs=pl.BlockSpec((tq, LANES), lambda b, h, i, f: (b * nq + i, h)),
        scratch_shapes=[pltpu.VMEM((ATT_VT_ROWS, S), BF16), pltpu.VMEM((2, ATT_VT_ROWS, tq), F32),
                        pltpu.VMEM((2, tq, 1), F32), pltpu.VMEM((2, tq, 1), F32),
                        pltpu.VMEM((2, tq, LANES), F32)])
    return pl.pallas_call(
        functools.partial(_attn_kernel, tk=tk, lam_init=lam_init),
        out_shape=jax.ShapeDtypeStruct((N, W), BF16),
        grid_spec=grid_spec,
        compiler_params=_cparams(("arbitrary", "arbitrary", "arbitrary")),
        name="attn",
    )(safe, qkv, qkv, qkv, lamp, sg)


def _split_bf16(x):
    hi = x.astype(BF16)
    lo = (x - hi.astype(F32)).astype(BF16)
    return hi, lo


def _route(sel, aff):
    epg = EXPERTS_PER_GROUP
    T = sel.shape[1]
    rows = [sel[e:e + 1, :] for e in range(N_EXPERTS)]
    arow = [aff[e:e + 1, :] for e in range(N_EXPERTS)]
    best = None
    gidx = None
    for g in range(N_GROUPS):
        r = rows[g * epg:(g + 1) * epg]
        gs = None
        for a in range(epg):
            for b in range(a + 1, epg):
                pair = r[a] + r[b]
                gs = pair if gs is None else jnp.maximum(gs, pair)
        if best is None:
            best, gidx = gs, jnp.zeros((1, T), jnp.int32)
        else:
            better = gs > best
            gidx = jnp.where(better, g, gidx)
            best = jnp.where(better, gs, best)
    ig, ag = [], []
    for a in range(epg):
        s_a, f_a = rows[a], arow[a]
        for g in range(1, N_GROUPS):
            pick = gidx == g
            s_a = jnp.where(pick, rows[g * epg + a], s_a)
            f_a = jnp.where(pick, arow[g * epg + a], f_a)
        ig.append(s_a)
        ag.append(f_a)
    v1, a1, w1 = ig[0], jnp.zeros((1, T), jnp.int32), ag[0]
    for a in range(1, epg):
        better = ig[a] > v1
        a1 = jnp.where(better, a, a1)
        w1 = jnp.where(better, ag[a], w1)
        v1 = jnp.where(better, ig[a], v1)
    v2 = jnp.full((1, T), -jnp.inf, F32)
    a2 = jnp.zeros((1, T), jnp.int32)
    w2 = jnp.zeros((1, T), F32)
    for a in range(epg):
        better = jnp.logical_and(a1 != a, ig[a] > v2)
        a2 = jnp.where(better, a, a2)
        w2 = jnp.where(better, ag[a], w2)
        v2 = jnp.where(better, ig[a], v2)
    tot = w1 + w2
    e1 = gidx * epg + a1
    e2 = gidx * epg + a2
    erow = lax.broadcasted_iota(jnp.int32, (N_EXPERTS, T), 0)
    comb = jnp.where(erow == e1, w1 / tot, 0.0) + jnp.where(erow == e2, w2 / tot, 0.0)
    return comb, gidx


def _merge_kernel(x_ref, bl_ref, bp_ref, ba_ref, gt_ref, mod_ref, wb_ref, wo_ref, g2_ref, rw_ref,
                  rb_ref, xo_ref, h2_ref, cb_ref, gr_ref):
    T, D = x_ref.shape
    m = mod_ref[0]
    w_hi, w_lo = _split_bf16(rw_ref[...])
    for half in range(MERGE_ROW_SPLITS):
        Th = T // MERGE_ROW_SPLITS
        rows = slice(half * Th, (half + 1) * Th)
        merged = None
        for n, br in enumerate((bl_ref, bp_ref, ba_ref)):
            gate = jax.nn.sigmoid(gt_ref[rows, n * D:(n + 1) * D].astype(F32))
            term = gate * _dot(br[rows, :], wb_ref[n])
            merged = term if merged is None else merged + term
        y = _dot(merged.astype(BF16), wo_ref[...])
        x = x_ref[rows, :] + m[2:3] * y
        xo_ref[rows, :] = x
        var = jnp.mean(x * x, axis=-1, keepdims=True)
        h2 = x * lax.rsqrt(var + EPS) * g2_ref[...] * (1.0 + m[4:5]) + m[3:4]
        h2_ref[rows, :] = h2.astype(BF16)
        h_hi, h_lo = _split_bf16(h2)
        logits = _dot(h_hi, w_hi) + _dot(h_hi, w_lo) + _dot(h_lo, w_hi)
        lt = logits.T[0:N_EXPERTS, :]
        aff = jax.nn.sigmoid(lt)
        comb, gidx = _route(aff + rb_ref[...], aff)
        comb = jnp.concatenate([comb, jnp.zeros((LANES - N_EXPERTS, Th), F32)], axis=0)
        cb_ref[rows, :] = comb.T
        gr_ref[:, rows] = jnp.broadcast_to(gidx.astype(F32), (SUBLANES, Th))


def _merge_call(x, br_l, br_p, br_a, gates, mod, wb, wo, g2, rw, rb, S, tm):
    N, D = x.shape
    tpb = S // tm
    tok = lambda: pl.BlockSpec((tm, D), lambda i: (i, 0))
    return pl.pallas_call(
        _merge_kernel,
        out_shape=(jax.ShapeDtypeStruct((N, D), F32), jax.ShapeDtypeStruct((N, D), BF16),
                   jax.ShapeDtypeStruct((N, LANES), F32), jax.ShapeDtypeStruct((SUBLANES, N), F32)),
        grid=(N // tm,),
        in_specs=[tok(), tok(), tok(), tok(),
                  pl.BlockSpec((tm, N_BRANCH * D), lambda i: (i, 0)),
                  pl.BlockSpec((1, SUBLANES, D), lambda i: (i // tpb, 0, 0)),
                  pl.BlockSpec((N_BRANCH, D, D), lambda i: (0, 0, 0)),
                  pl.BlockSpec((D, D), lambda i: (0, 0)),
                  pl.BlockSpec((1, D), lambda i: (0, 0)),
                  pl.BlockSpec((D, LANES), lambda i: (0, 0)),
                  pl.BlockSpec((N_EXPERTS, 1), lambda i: (0, 0))],
        out_specs=(tok(), tok(), pl.BlockSpec((tm, LANES), lambda i: (i, 0)),
                   pl.BlockSpec((SUBLANES, tm), lambda i: (0, i))),
        compiler_params=_cparams(("arbitrary",)),
        name="merge",
    )(x, br_l, br_p, br_a, gates, mod, wb, wo, g2, rw, rb)


def _moe_kernel(h_ref, cb_ref, gr_ref, x_ref, mod_ref, wg_ref, wu_ref, wd_ref, o_ref,
                dm_s, xs_s, cw_s, ys_s, seg_s):
    T, D = h_ref.shape
    R = MOE_CHUNK_ROWS
    e = pl.program_id(1)

    @pl.when(e == 0)
    def _():
        gid = gr_ref[0:1, :]
        grow = lax.broadcasted_iota(jnp.int32, (2 * SUBLANES, T), 0).astype(F32)
        oh = grow == gid
        ohf = jnp.where(oh, 1.0, 0.0)
        ohb = ohf.astype(BF16)
        rb = MOE_RANK_BLOCK
        tri = jnp.where(lax.broadcasted_iota(jnp.int32, (rb, rb), 0)
                        < lax.broadcasted_iota(jnp.int32, (rb, rb), 1), 1.0, 0.0).astype(BF16)
        count = jnp.zeros((2 * SUBLANES, 1), F32)
        ranks = []
        for b in range(T // rb):
            cols = slice(b * rb, (b + 1) * rb)
            ranks.append(_dot(ohb[:, cols], tri) + count)
            count = count + jnp.sum(ohf[:, cols], axis=1, keepdims=True)
        rank = jnp.concatenate(ranks, axis=1)
        rowi = lax.broadcasted_iota(jnp.int32, (2 * SUBLANES, 1), 0)
        start = jnp.zeros((2 * SUBLANES, 1), F32)
        run = jnp.zeros((1, 1), F32)
        for g in range(N_GROUPS):
            start = jnp.where(rowi == g, run, start)
            seg_s[g] = run[0, 0].astype(jnp.int32)
            run = run + count[g:g + 1, :]
        seg_s[N_GROUPS] = run[0, 0].astype(jnp.int32)
        pos = jnp.sum(jnp.where(oh, start + rank, 0.0), axis=0, keepdims=True).astype(jnp.int32)
        dm = jnp.where(lax.broadcasted_iota(jnp.int32, (T, T), 0) == pos, 1.0, 0.0).astype(BF16)
        dm_s[...] = dm
        c_hi, c_lo = _split_bf16(cb_ref[...])
        moved = _dot(dm, jnp.concatenate([h_ref[...], c_hi, c_lo], axis=1))
        xs_s[...] = moved[:, 0:D].astype(BF16)
        cw_s[...] = moved[:, D:D + LANES] + moved[:, D + LANES:D + 2 * LANES]
        ys_s[...] = jnp.zeros_like(ys_s)

    g = (e * MOE_EXPERTS_PER_STEP) // EXPERTS_PER_GROUP
    lo = seg_s[g]
    hi = seg_s[g + 1]
    lane = lax.broadcasted_iota(jnp.int32, (R, LANES), 1)

    def chunk(c, carry):
        rows = pl.ds(pl.multiple_of(c * R, R), R)
        xc = xs_s[rows, :]
        cws = cw_s[rows, :]
        y = None
        for s in range(MOE_EXPERTS_PER_STEP):
            gg = _dot(xc, wg_ref[s])
            uu = _dot(xc, wu_ref[s])
            cw = jnp.sum(jnp.where(lane == e * MOE_EXPERTS_PER_STEP + s, cws, 0.0), axis=-1, keepdims=True)
            act = (gg * jax.nn.sigmoid(gg)) * uu * cw
            part = _dot(act.astype(BF16), wd_ref[s])
            y = part if y is None else y + part
        ys_s[rows, :] += y
        return carry

    lax.fori_loop(lo // R, (hi + R - 1) // R, chunk, 0)

    @pl.when(e == pl.num_programs(1) - 1)
    def _():
        y = lax.dot_general(dm_s[...], ys_s[...].astype(BF16), (((0,), (0,)), ((), ())),
                            preferred_element_type=F32)
        o_ref[...] = x_ref[...] + mod_ref[0][5:6] * y


def _moe_call(h2, comb, grp, x, mod, wg, wu, wd, S, tm):
    N, D = x.shape
    E, _, F = wg.shape
    tpb = S // tm
    return pl.pallas_call(
        _moe_kernel,
        out_shape=jax.ShapeDtypeStruct((N, D), F32),
        grid=(N // tm, E // MOE_EXPERTS_PER_STEP),
        in_specs=[pl.BlockSpec((tm, D), lambda i, e: (i, 0)),
                  pl.BlockSpec((tm, LANES), lambda i, e: (i, 0)),
                  pl.BlockSpec((SUBLANES, tm), lambda i, e: (0, i)),
                  pl.BlockSpec((tm, D), lambda i, e: (i, 0)),
                  pl.BlockSpec((1, SUBLANES, D), lambda i, e: (i // tpb, 0, 0)),
                  pl.BlockSpec((MOE_EXPERTS_PER_STEP, D, F), lambda i, e: (e, 0, 0)),
                  pl.BlockSpec((MOE_EXPERTS_PER_STEP, D, F), lambda i, e: (e, 0, 0)),
                  pl.BlockSpec((MOE_EXPERTS_PER_STEP, F, D), lambda i, e: (e, 0, 0))],
        out_specs=pl.BlockSpec((tm, D), lambda i, e: (i, 0)),
        scratch_shapes=[pltpu.VMEM((tm, tm), BF16), pltpu.VMEM((tm, D), BF16),
                        pltpu.VMEM((tm, LANES), F32), pltpu.VMEM((tm, D), F32),
                        pltpu.SMEM((SUBLANES,), jnp.int32)],
        compiler_params=_cparams(("arbitrary", "arbitrary")),
        name="moe",
    )(h2, comb, grp, x, mod, wg, wu, wd)


def _tiles(S):
    t = lambda want: min(want, S)
    return dict(inproj=t(512), lru=t(256), pool=t(256), qk=t(512), tq=t(1024), tk=t(1024),
                merge=t(512), moe=t(1024))


def kernel(x, c, w_ada, b_ada, norm1_g, norm2_g, w_in, conv_w, conv_b, lru_wa, lru_ba, lru_wx, lru_bx,
           lru_l, pool_w, pool_scale, qn_g, kn_g, lam_q1, lam_k1, lam_q2, lam_k2, subln_g, w_branch,
           w_out, router_w, router_b, w_gate, w_up, w_down):
    B, S, D = x.shape
    L = w_ada.shape[0]
    N = B * S
    W = D
    ts = _tiles(S)
    mod = _mod_call(c, w_ada, b_ada)
    cs, sn = _rope_tables(S)
    rw = jnp.pad(router_w, ((0, 0), (0, LANES - N_EXPERTS)))
    rb = router_b.reshape(N_EXPERTS, 1)
    hd = ATT_HEAD_DIM
    xf = x.reshape(N, D)
    for l in range(L):
        lam_init = 0.8 - 0.6 * math.exp(-0.3 * l)
        w_l = w_in[l].astype(BF16)
        g1 = norm1_g[l].reshape(1, D)
        qg = jnp.tile(qn_g[l], 2).reshape(1, LANES)
        kg = jnp.tile(kn_g[l], 2).reshape(1, LANES)
        proj = _inproj_call(xf, mod[l], g1, w_l[:, 0:3 * D], S, ts["inproj"])
        qkv = _inproj_call(xf, mod[l], g1, w_l[:, 3 * D:6 * D], S, ts["inproj"], rope=(cs, sn, qg, kg))
        gates = _inproj_call(xf, mod[l], g1, w_l[:, 6 * D:9 * D], S, ts["inproj"])
        br_l = _lru_call(proj, conv_w[l], conv_b[l].reshape(1, W), lru_wa[l].astype(BF16),
                         lru_ba[l].reshape(1, W), lru_wx[l].astype(BF16), lru_bx[l].reshape(1, W),
                         lru_l[l].reshape(1, W), S, W, ts["lru"])
        br_p = _pool_call(proj, pool_w[l].astype(BF16), pool_scale[l].reshape(1, W), S, W, ts["pool"])
        lamp = jnp.zeros((SUBLANES, LANES), F32)
        lamp = lamp.at[0, :hd].set(lam_q1[l]).at[1, :hd].set(lam_k1[l])
        lamp = lamp.at[2, :hd].set(lam_q2[l]).at[3, :hd].set(lam_k2[l])
        bound = math.sqrt(hd) * jnp.max(jnp.abs(qn_g[l])) * jnp.max(jnp.abs(kn_g[l]))
        safe = (bound < SAFE_SCORE_BOUND).astype(jnp.int32).reshape(1)
        br_a = _attn_call(qkv, safe, lamp, subln_g[l].reshape(1, LANES), B, S, lam_init,
                          ts["tq"], ts["tk"])
        xf, h2, comb, grp = _merge_call(xf, br_l, br_p, br_a, gates, mod[l], w_branch[l].astype(BF16),
                                        w_out[l].astype(BF16), norm2_g[l].reshape(1, D), rw, rb, S,
                                        ts["merge"])
        xf = _moe_call(h2, comb, grp, xf, mod[l], w_gate[l].astype(BF16), w_up[l].astype(BF16),
                       w_down[l].astype(BF16), S, ts["moe"])
    return xf.reshape(B, S, D)
```

```python
import functools
import math

import jax
import jax.numpy as jnp
from jax import lax
from jax.experimental import pallas as pl
from jax.experimental.pallas import tpu as pltpu

F32 = jnp.float32
BF16 = jnp.bfloat16

EPS = 1e-6
LANES = 128
SUBLANES = 8
LRU_BLOCKS = 8
CONV_WIDTH = 4
LRU_C = 8.0
POOL_WINDOWS = (2, 4, 8, 16)
POOL_HALO = 16
ATT_HEADS = 8
ATT_HEAD_DIM = 64
QK_BLOCK_COLS = 512
ATT_KV_UNROLL = 2
ATT_VT_ROWS = 144
ROPE_THETA = 10000.0
N_BRANCH = 3
N_EXPERTS = 16
N_GROUPS = 4
EXPERTS_PER_GROUP = N_EXPERTS // N_GROUPS
MERGE_ROW_SPLITS = 2
MOE_CHUNK_ROWS = 128
MOE_EXPERTS_PER_STEP = 2
MOE_RANK_BLOCK = 256
VMEM_LIMIT = 56 * 1024 * 1024
LOG2E = math.log2(math.e)
SAFE_SCORE_BOUND = 32.0


def _cparams(sem):
    return pltpu.CompilerParams(dimension_semantics=sem, vmem_limit_bytes=VMEM_LIMIT)


def _dot(a, b):
    return jnp.dot(a, b, preferred_element_type=F32)


def _dot_nt(a, b):
    return lax.dot_general(a, b, (((1,), (1,)), ((), ())), preferred_element_type=F32)


def _mod_kernel(c_ref, w_ref, b_ref, o_ref):
    c = c_ref[...]
    ca = c * jax.nn.sigmoid(c)
    o_ref[0] = jnp.dot(ca, w_ref[0], preferred_element_type=F32,
                       precision=lax.Precision.HIGHEST) + b_ref[0]


def _mod_call(c, w_ada, b_ada):
    L, D, D6 = w_ada.shape
    B = c.shape[0]
    cp = jnp.zeros((SUBLANES, D), F32).at[:B].set(c)
    out = pl.pallas_call(
        _mod_kernel,
        out_shape=jax.ShapeDtypeStruct((L, SUBLANES, D6), F32),
        grid=(L, D6 // D),
        in_specs=[pl.BlockSpec((SUBLANES, D), lambda l, j: (0, 0)),
                  pl.BlockSpec((1, D, D), lambda l, j: (l, 0, j)),
                  pl.BlockSpec((1, 1, D), lambda l, j: (l, 0, j))],
        out_specs=pl.BlockSpec((1, SUBLANES, D), lambda l, j: (l, 0, j)),
        compiler_params=_cparams(("arbitrary", "arbitrary")),
        name="mod",
    )(cp, w_ada, b_ada.reshape(L, 1, D6))
    mod = out[:, :B].reshape(L, B, 6, D)
    return jnp.pad(mod, ((0, 0), (0, 0), (0, 2), (0, 0)))


def _qk_norm_rope(x, ssq, g, cs, sn, lower, scale):
    hd = ATT_HEAD_DIM
    xn = x * lax.rsqrt(ssq * (1.0 / hd) + EPS) * g
    partner = jnp.where(lower, pltpu.roll(xn, LANES - hd // 2, axis=1), pltpu.roll(xn, hd // 2, axis=1))
    return (xn * cs + partner * sn) * scale


def _inproj_kernel(x_ref, mod_ref, g_ref, w_ref, *rest, qk_cols):
    o_ref = rest[-1]
    x = x_ref[...]
    m = mod_ref[0]
    var = jnp.mean(x * x, axis=-1, keepdims=True)
    y = x * lax.rsqrt(var + EPS) * g_ref[...]
    h = y * (1.0 + m[1:2]) + m[0:1]
    hb = h.astype(BF16)
    if not qk_cols:
        o_ref[...] = _dot(hb, w_ref[...]).astype(BF16)
        return
    cs_ref, sn_ref, qg_ref, kg_ref = rest[:4]
    T = x.shape[0]
    lane = lax.broadcasted_iota(jnp.int32, (T, LANES), 1)
    lower = (lane % ATT_HEAD_DIM) < (ATT_HEAD_DIM // 2)
    cs, sn = cs_ref[...], sn_ref[...]
    half = qk_cols // 2
    bw = QK_BLOCK_COLS
    comp = (lax.broadcasted_iota(jnp.int32, (bw, bw), 0) // ATT_HEAD_DIM
            == lax.broadcasted_iota(jnp.int32, (bw, bw), 1) // ATT_HEAD_DIM)
    comp_ones = jnp.where(comp, 1.0, 0.0).astype(BF16)
    for c0 in range(0, o_ref.shape[1], bw):
        res = _dot(hb, w_ref[:, c0:c0 + bw])
        if c0 >= qk_cols:
            o_ref[:, c0:c0 + bw] = res.astype(BF16)
            continue
        is_q = c0 < half
        g = qg_ref[...] if is_q else kg_ref[...]
        scale = ATT_HEAD_DIM ** -0.5 * LOG2E if is_q else 1.0
        ssq = _dot((res * res).astype(BF16), comp_ones)
        for c in range(0, bw, LANES):
            blk = _qk_norm_rope(res[:, c:c + LANES], ssq[:, c:c + LANES], g, cs, sn, lower, scale)
            o_ref[:, c0 + c:c0 + c + LANES] = blk.astype(BF16)


def _inproj_call(x, mod, g, w, S, tm, rope=None):
    N, D = x.shape
    W = w.shape[1]
    tpb = S // tm
    in_specs = [pl.BlockSpec((tm, D), lambda i: (i, 0)),
                pl.BlockSpec((1, SUBLANES, D), lambda i: (i // tpb, 0, 0)),
                pl.BlockSpec((1, D), lambda i: (0, 0)),
                pl.BlockSpec((D, W), lambda i: (0, 0))]
    args = [x, mod, g, w]
    if rope is not None:
        in_specs += [pl.BlockSpec((tm, LANES), lambda i: (i % tpb, 0)),
                     pl.BlockSpec((tm, LANES), lambda i: (i % tpb, 0)),
                     pl.BlockSpec((1, LANES), lambda i: (0, 0)),
                     pl.BlockSpec((1, LANES), lambda i: (0, 0))]
        args += list(rope)
    return pl.pallas_call(
        functools.partial(_inproj_kernel, qk_cols=0 if rope is None else 2 * W // 3),
        out_shape=jax.ShapeDtypeStruct((N, W), BF16),
        grid=(N // tm,),
        in_specs=in_specs,
        out_specs=pl.BlockSpec((tm, W), lambda i: (i, 0)),
        compiler_params=_cparams(("arbitrary",)),
        name="inproj_qkv" if rope is not None else "inproj",
    )(*args)


def _lru_kernel(x_ref, y_ref, cw_ref, cb_ref, wa_ref, ba_ref, wx_ref, bx_ref, ll_ref, o_ref,
                xbuf, hcar, *, tpb):
    T, W = x_ref.shape
    blk = W // LRU_BLOCKS
    i = pl.program_id(0)

    @pl.when(i % tpb == 0)
    def _():
        xbuf[0:SUBLANES, :] = jnp.zeros((SUBLANES, W), F32)
        hcar[...] = jnp.zeros_like(hcar)

    xbuf[SUBLANES:SUBLANES + T, :] = x_ref[...].astype(F32)
    xc = cb_ref[...] + xbuf[SUBLANES:SUBLANES + T, :] * cw_ref[CONV_WIDTH - 1:CONV_WIDTH, :]
    for j in range(CONV_WIDTH - 1):
        d = CONV_WIDTH - 1 - j
        xc = xc + xbuf[SUBLANES - d:SUBLANES - d + T, :] * cw_ref[j:j + 1, :]
    xbuf[0:SUBLANES, :] = xbuf[T:T + SUBLANES, :]

    xcb = xc.astype(BF16)
    ga, gx = [], []
    for n in range(LRU_BLOCKS):
        xs = xcb[:, n * blk:(n + 1) * blk]
        ga.append(_dot(xs, wa_ref[n]))
        gx.append(_dot(xs, wx_ref[n]))
    gate_a = jax.nn.sigmoid(jnp.concatenate(ga, axis=1) + ba_ref[...])
    gate_x = jax.nn.sigmoid(jnp.concatenate(gx, axis=1) + bx_ref[...])
    ll = ll_ref[...]
    log_sig = jnp.minimum(ll, 0.0) - jnp.log1p(jnp.exp(-jnp.abs(ll)))
    log_a = LRU_C * gate_a * log_sig
    a = jnp.exp(log_a)
    om = 1.0 - a * a
    u = om * lax.rsqrt(jnp.maximum(om, 1e-30)) * (gate_x * xc)

    G = T // SUBLANES
    A = a.reshape(G, SUBLANES, W)
    H = u.reshape(G, SUBLANES, W)
    sub = lax.broadcasted_iota(jnp.int32, (G, SUBLANES, W), 1)
    sh = 1
    while sh < SUBLANES:
        keep = sub >= sh
        A_s = jnp.where(keep, pltpu.roll(A, sh, axis=1), 1.0)
        H_s = jnp.where(keep, pltpu.roll(H, sh, axis=1), 0.0)
        H = A * H_s + H
        A = A * A_s
        sh *= 2
    gy = jax.nn.gelu(y_ref[...].astype(F32))
    carry = hcar[...]
    hs = []
    for g in range(G):
        hs.append(H[g] + A[g] * carry)
        carry = H[g, SUBLANES - 1:SUBLANES, :] + A[g, SUBLANES - 1:SUBLANES, :] * carry
        if g % 2 == 1:
            rows = slice((g - 1) * SUBLANES, (g + 1) * SUBLANES)
            h = jnp.concatenate(hs, axis=0)
            o_ref[rows, :] = (h * gy[rows, :]).astype(o_ref.dtype)
            hs = []
    hcar[...] = carry


def _lru_call(proj, cw, cb, wa, ba, wx, bx, ll, S, W, tm):
    N = proj.shape[0]
    tpb = S // tm
    blk = W // LRU_BLOCKS
    vec = lambda: pl.BlockSpec((1, W), lambda i: (0, 0))
    return pl.pallas_call(
        functools.partial(_lru_kernel, tpb=tpb),
        out_shape=jax.ShapeDtypeStruct((N, W), BF16),
        grid=(N // tm,),
        in_specs=[pl.BlockSpec((tm, W), lambda i: (i, 0)),
                  pl.BlockSpec((tm, W), lambda i: (i, 1)),
                  pl.BlockSpec((CONV_WIDTH, W), lambda i: (0, 0)),
                  vec(),
                  pl.BlockSpec((LRU_BLOCKS, blk, blk), lambda i: (0, 0, 0)),
                  vec(),
                  pl.BlockSpec((LRU_BLOCKS, blk, blk), lambda i: (0, 0, 0)),
                  vec(),
                  vec()],
        out_specs=pl.BlockSpec((tm, W), lambda i: (i, 0)),
        scratch_shapes=[pltpu.VMEM((tm + SUBLANES, W), F32), pltpu.VMEM((1, W), F32)],
        compiler_params=_cparams(("arbitrary",)),
        name="lru",
    )(proj, proj, cw, cb, wa, ba, wx, bx, ll)


def _pool_kernel(x_ref, pw_ref, ps_ref, o_ref, xbuf, *, tpb):
    T, W = x_ref.shape
    G = len(POOL_WINDOWS)
    gw = W // G
    i = pl.program_id(0)

    @pl.when(i % tpb == 0)
    def _():
        xbuf[0:POOL_HALO, :] = jnp.zeros((POOL_HALO, W), F32)

    xbuf[POOL_HALO:POOL_HALO + T, :] = x_ref[...].astype(F32)
    t = (i % tpb) * T + lax.broadcasted_iota(jnp.int32, (T, 1), 0)
    outs = []
    for g, w in enumerate(POOL_WINDOWS):
        cols = slice(g * gw, (g + 1) * gw)
        xg = xbuf[POOL_HALO:POOL_HALO + T, cols]
        ws = xg
        for d in range(1, w):
            ws = ws + xbuf[POOL_HALO - d:POOL_HALO - d + T, cols]
        cnt = jnp.minimum(t + 1, w).astype(F32)
        pooled = (ws / cnt - xg).astype(BF16)
        outs.append(_dot(pooled, pw_ref[g]))
    xbuf[0:POOL_HALO, :] = xbuf[T:T + POOL_HALO, :]
    o_ref[...] = (jnp.concatenate(outs, axis=1) * ps_ref[...]).astype(o_ref.dtype)


def _pool_call(proj, pw, ps, S, W, tm):
    N = proj.shape[0]
    tpb = S // tm
    G = len(POOL_WINDOWS)
    return pl.pallas_call(
        functools.partial(_pool_kernel, tpb=tpb),
        out_shape=jax.ShapeDtypeStruct((N, W), BF16),
        grid=(N // tm,),
        in_specs=[pl.BlockSpec((tm, W), lambda i: (i, 2)),
                  pl.BlockSpec((G, W // G, W // G), lambda i: (0, 0, 0)),
                  pl.BlockSpec((1, W), lambda i: (0, 0))],
        out_specs=pl.BlockSpec((tm, W), lambda i: (i, 0)),
        scratch_shapes=[pltpu.VMEM((tm + POOL_HALO, W), F32)],
        compiler_params=_cparams(("arbitrary",)),
        name="pool",
    )(proj, pw, ps)


def _rope_tables(S):
    hd = ATT_HEAD_DIM
    pos = jnp.arange(S, dtype=F32)
    inv = ROPE_THETA ** (-jnp.arange(0, hd, 2, dtype=F32) / hd)
    ang = pos[:, None] * inv[None, :]
    c, s = jnp.cos(ang), jnp.sin(ang)
    cs = jnp.concatenate([c, c, c, c], axis=1)
    sn = jnp.concatenate([-s, s, -s, s], axis=1)
    return cs, sn


def _attn_kernel(safe_ref, q_ref, k_ref, v_ref, lp_ref, sg_ref, o_ref, vt_s, acct_s, m_s, l_s, acc_s, *,
                 tk, lam_init):
    tq = q_ref.shape[0]
    S = k_ref.shape[0]
    i = pl.program_id(2)
    lane = lax.broadcasted_iota(jnp.int32, (tq, LANES), 1)
    q = q_ref[...]
    zero = jnp.zeros_like(q)
    qc = (jnp.where(lane < ATT_HEAD_DIM, q, zero), jnp.where(lane < ATT_HEAD_DIM, zero, q))
    safe = safe_ref[0] == 1
    n_full = (i * tq) // tk
    n_all = ((i + 1) * tq + tk - 1) // tk

    def scores(c, j, k, masked):
        s = _dot_nt(qc[c], k)
        if masked:
            qpos = i * tq + lax.broadcasted_iota(jnp.int32, (tq, tk), 0)
            kpos = j * tk + lax.broadcasted_iota(jnp.int32, (tq, tk), 1)
            s = jnp.where(kpos <= qpos, s, -jnp.inf)
        return s

    def loops(tile, unroll=1):
        first_single = 0
        if unroll > 1:
            n_groups = n_full // unroll

            def group(jj, c):
                for u in range(unroll):
                    tile(unroll * jj + u, False)
                return c

            lax.fori_loop(0, n_groups, group, 0)
            first_single = unroll * n_groups
        lax.fori_loop(first_single, n_full, lambda j, c: tile(j, False) or c, 0)
        lax.fori_loop(n_full, n_all, lambda j, c: tile(j, True) or c, 0)

    def finalize(a1, l1, a2, l2):
        lp = lp_ref[...]
        lam = (jnp.exp(jnp.sum(lp[0:1] * lp[1:2], axis=-1, keepdims=True))
               - jnp.exp(jnp.sum(lp[2:3] * lp[3:4], axis=-1, keepdims=True)) + lam_init)
        o = a1 / l1 - lam * (a2 / l2)
        o = o * lax.rsqrt(jnp.mean(o * o, axis=-1, keepdims=True) + EPS) * sg_ref[...]
        o_ref[...] = (o * (1.0 - lam_init)).astype(o_ref.dtype)

    @pl.when(jnp.logical_and(safe, i == 0))
    def _():
        tail = jnp.where(lax.broadcasted_iota(jnp.int32, (ATT_VT_ROWS - LANES, tk), 0) == 0,
                         1.0, 0.0).astype(BF16)

        def fill(j, c):
            off = pl.multiple_of(j * tk, tk)
            vt_s[0:LANES, pl.ds(off, tk)] = v_ref[pl.ds(off, tk), :].astype(F32).T.astype(BF16)
            vt_s[LANES:ATT_VT_ROWS, pl.ds(off, tk)] = tail
            return c

        lax.fori_loop(0, S // tk, fill, 0)

    @pl.when(safe)
    def _():
        acct_s[...] = jnp.zeros_like(acct_s)

        def tile(j, masked):
            off = pl.multiple_of(j * tk, tk)
            k = k_ref[pl.ds(off, tk), :]
            vt = vt_s[:, pl.ds(off, tk)]
            if masked and tq == tk:
                hq = tq // 2
                tri = (lax.broadcasted_iota(jnp.int32, (hq, hq), 0)
                       <= lax.broadcasted_iota(jnp.int32, (hq, hq), 1))
                for c in range(2):
                    s_a = jnp.where(tri, _dot_nt(k[0:hq], qc[c][0:hq]), -jnp.inf)
                    acct_s[c, :, 0:hq] += _dot(vt[:, 0:hq], jnp.exp2(s_a).astype(BF16))
                    s_b = _dot_nt(k, qc[c][hq:tq])
                    s_b = jnp.concatenate([s_b[0:hq], jnp.where(tri, s_b[hq:tk], -jnp.inf)], axis=0)
                    acct_s[c, :, hq:tq] += _dot(vt, jnp.exp2(s_b).astype(BF16))
                return
            for c in range(2):
                s = _dot_nt(k, qc[c])
                if masked:
                    kpos = j * tk + lax.broadcasted_iota(jnp.int32, (tk, tq), 0)
                    qpos = i * tq + lax.broadcasted_iota(jnp.int32, (tk, tq), 1)
                    s = jnp.where(kpos <= qpos, s, -jnp.inf)
                acct_s[c] += _dot(vt, jnp.exp2(s).astype(BF16))

        loops(tile, unroll=ATT_KV_UNROLL)
        a1, a2 = acct_s[0], acct_s[1]
        lp = lp_ref[...]
        lam = (jnp.exp(jnp.sum(lp[0:1] * lp[1:2], axis=-1, keepdims=True))
               - jnp.exp(jnp.sum(lp[2:3] * lp[3:4], axis=-1, keepdims=True)) + lam_init)
        o = a1[0:LANES] / a1[LANES:LANES + 1] - lam * (a2[0:LANES] / a2[LANES:LANES + 1])
        o = o * lax.rsqrt(jnp.mean(o * o, axis=0, keepdims=True) + EPS)
        o_ref[...] = (o.T * sg_ref[...] * (1.0 - lam_init)).astype(o_ref.dtype)

    @pl.when(jnp.logical_not(safe))
    def _():
        m_s[...] = jnp.full_like(m_s, -jnp.inf)
        l_s[...] = jnp.zeros_like(l_s)
        acc_s[...] = jnp.zeros_like(acc_s)

        def tile(j, masked):
            off = pl.multiple_of(j * tk, tk)
            k = k_ref[pl.ds(off, tk), :]
            v = v_ref[pl.ds(off, tk), :]
            for c in range(2):
                s = scores(c, j, k, masked)
                m_old = m_s[c]
                m_new = jnp.maximum(m_old, jnp.max(s, axis=-1, keepdims=True))
                alpha = jnp.exp2(m_old - m_new)
                p = jnp.exp2(s - m_new)
                l_s[c] = alpha * l_s[c] + jnp.sum(p, axis=-1, keepdims=True)
                acc_s[c] = alpha * acc_s[c] + _dot(p.astype(BF16), v)
                m_s[c] = m_new

        loops(tile)
        finalize(acc_s[0], l_s[0], acc_s[1], l_s[1])


def _attn_call(qkv, safe, lamp, sg, B, S, lam_init, tq, tk):
    N = qkv.shape[0]
    W = qkv.shape[1] // 3
    H = W // LANES
    nq = S // tq
    grid_spec = pltpu.PrefetchScalarGridSpec(
        num_scalar_prefetch=1,
        grid=(B, H, nq),
        in_specs=[pl.BlockSpec((tq, LANES), lambda b, h, i, f: (b * nq + i, h)),
                  pl.BlockSpec((S, LANES), lambda b, h, i, f: (b, H + h)),
                  pl.BlockSpec((S, LANES), lambda b, h, i, f: (b, 2 * H + h)),
                  pl.BlockSpec((SUBLANES, LANES), lambda b, h, i, f: (0, 0)),
                  pl.BlockSpec((1, LANES), lambda b, h, i, f: (0, 0))],
        out_specs=pl.BlockSpec((tq, LANES), lambda b, h, i, f: (b * nq + i, h)),
        scratch_shapes=[pltpu.VMEM((ATT_VT_ROWS, S), BF16), pltpu.VMEM((2, ATT_VT_ROWS, tq), F32),
                        pltpu.VMEM((2, tq, 1), F32), pltpu.VMEM((2, tq, 1), F32),
                        pltpu.VMEM((2, tq, LANES), F32)])
    return pl.pallas_call(
        functools.partial(_attn_kernel, tk=tk, lam_init=lam_init),
        out_shape=jax.ShapeDtypeStruct((N, W), BF16),
        grid_spec=grid_spec,
        compiler_params=_cparams(("arbitrary", "arbitrary", "arbitrary")),
        name="attn",
    )(safe, qkv, qkv, qkv, lamp, sg)


def _split_bf16(x):
    hi = x.astype(BF16)
    lo = (x - hi.astype(F32)).astype(BF16)
    return hi, lo


def _route(sel, aff):
    epg = EXPERTS_PER_GROUP
    T = sel.shape[1]
    rows = [sel[e:e + 1, :] for e in range(N_EXPERTS)]
    arow = [aff[e:e + 1, :] for e in range(N_EXPERTS)]
    best = None
    gidx = None
    for g in range(N_GROUPS):
        r = rows[g * epg:(g + 1) * epg]
        gs = None
        for a in range(epg):
            for b in range(a + 1, epg):
                pair = r[a] + r[b]
                gs = pair if gs is None else jnp.maximum(gs, pair)
        if best is None:
            best, gidx = gs, jnp.zeros((1, T), jnp.int32)
        else:
            better = gs > best
            gidx = jnp.where(better, g, gidx)
            best = jnp.where(better, gs, best)
    ig, ag = [], []
    for a in range(epg):
        s_a, f_a = rows[a], arow[a]
        for g in range(1, N_GROUPS):
            pick = gidx == g
            s_a = jnp.where(pick, rows[g * epg + a], s_a)
            f_a = jnp.where(pick, arow[g * epg + a], f_a)
        ig.append(s_a)
        ag.append(f_a)
    v1, a1, w1 = ig[0], jnp.zeros((1, T), jnp.int32), ag[0]
    for a in range(1, epg):
        better = ig[a] > v1
        a1 = jnp.where(better, a, a1)
        w1 = jnp.where(better, ag[a], w1)
        v1 = jnp.where(better, ig[a], v1)
    v2 = jnp.full((1, T), -jnp.inf, F32)
    a2 = jnp.zeros((1, T), jnp.int32)
    w2 = jnp.zeros((1, T), F32)
    for a in range(epg):
        better = jnp.logical_and(a1 != a, ig[a] > v2)
        a2 = jnp.where(better, a, a2)
        w2 = jnp.where(better, ag[a], w2)
        v2 = jnp.where(better, ig[a], v2)
    tot = w1 + w2
    e1 = gidx * epg + a1
    e2 = gidx * epg + a2
    erow = lax.broadcasted_iota(jnp.int32, (N_EXPERTS, T), 0)
    comb = jnp.where(erow == e1, w1 / tot, 0.0) + jnp.where(erow == e2, w2 / tot, 0.0)
    return comb, gidx


def _merge_kernel(x_ref, bl_ref, bp_ref, ba_ref, gt_ref, mod_ref, wb_ref, wo_ref, g2_ref, rw_ref,
                  rb_ref, xo_ref, h2_ref, cb_ref, gr_ref):
    T, D = x_ref.shape
    m = mod_ref[0]
    w_hi, w_lo = _split_bf16(rw_ref[...])
    for half in range(MERGE_ROW_SPLITS):
        Th = T // MERGE_ROW_SPLITS
        rows = slice(half * Th, (half + 1) * Th)
        merged = None
        for n, br in enumerate((bl_ref, bp_ref, ba_ref)):
            gate = jax.nn.sigmoid(gt_ref[rows, n * D:(n + 1) * D].astype(F32))
            term = gate * _dot(br[rows, :], wb_ref[n])
            merged = term if merged is None else merged + term
        y = _dot(merged.astype(BF16), wo_ref[...])
        x = x_ref[rows, :] + m[2:3] * y
        xo_ref[rows, :] = x
        var = jnp.mean(x * x, axis=-1, keepdims=True)
        h2 = x * lax.rsqrt(var + EPS) * g2_ref[...] * (1.0 + m[4:5]) + m[3:4]
        h2_ref[rows, :] = h2.astype(BF16)
        h_hi, h_lo = _split_bf16(h2)
        logits = _dot(h_hi, w_hi) + _dot(h_hi, w_lo) + _dot(h_lo, w_hi)
        lt = logits.T[0:N_EXPERTS, :]
        aff = jax.nn.sigmoid(lt)
        comb, gidx = _route(aff + rb_ref[...], aff)
        comb = jnp.concatenate([comb, jnp.zeros((LANES - N_EXPERTS, Th), F32)], axis=0)
        cb_ref[rows, :] = comb.T
        gr_ref[:, rows] = jnp.broadcast_to(gidx.astype(F32), (SUBLANES, Th))


def _merge_call(x, br_l, br_p, br_a, gates, mod, wb, wo, g2, rw, rb, S, tm):
    N, D = x.shape
    tpb = S // tm
    tok = lambda: pl.BlockSpec((tm, D), lambda i: (i, 0))
    return pl.pallas_call(
        _merge_kernel,
        out_shape=(jax.ShapeDtypeStruct((N, D), F32), jax.ShapeDtypeStruct((N, D), BF16),
                   jax.ShapeDtypeStruct((N, LANES), F32), jax.ShapeDtypeStruct((SUBLANES, N), F32)),
        grid=(N // tm,),
        in_specs=[tok(), tok(), tok(), tok(),
                  pl.BlockSpec((tm, N_BRANCH * D), lambda i: (i, 0)),
                  pl.BlockSpec((1, SUBLANES, D), lambda i: (i // tpb, 0, 0)),
                  pl.BlockSpec((N_BRANCH, D, D), lambda i: (0, 0, 0)),
                  pl.BlockSpec((D, D), lambda i: (0, 0)),
                  pl.BlockSpec((1, D), lambda i: (0, 0)),
                  pl.BlockSpec((D, LANES), lambda i: (0, 0)),
                  pl.BlockSpec((N_EXPERTS, 1), lambda i: (0, 0))],
        out_specs=(tok(), tok(), pl.BlockSpec((tm, LANES), lambda i: (i, 0)),
                   pl.BlockSpec((SUBLANES, tm), lambda i: (0, i))),
        compiler_params=_cparams(("arbitrary",)),
        name="merge",
    )(x, br_l, br_p, br_a, gates, mod, wb, wo, g2, rw, rb)


def _moe_kernel(h_ref, cb_ref, gr_ref, x_ref, mod_ref, wg_ref, wu_ref, wd_ref, o_ref,
                dm_s, xs_s, cw_s, ys_s, seg_s):
    T, D = h_ref.shape
    R = MOE_CHUNK_ROWS
    e = pl.program_id(1)

    @pl.when(e == 0)
    def _():
        gid = gr_ref[0:1, :]
        grow = lax.broadcasted_iota(jnp.int32, (2 * SUBLANES, T), 0).astype(F32)
        oh = grow == gid
        ohf = jnp.where(oh, 1.0, 0.0)
        ohb = ohf.astype(BF16)
        rb = MOE_RANK_BLOCK
        tri = jnp.where(lax.broadcasted_iota(jnp.int32, (rb, rb), 0)
                        < lax.broadcasted_iota(jnp.int32, (rb, rb), 1), 1.0, 0.0).astype(BF16)
        count = jnp.zeros((2 * SUBLANES, 1), F32)
        ranks = []
        for b in range(T // rb):
            cols = slice(b * rb, (b + 1) * rb)
            ranks.append(_dot(ohb[:, cols], tri) + count)
            count = count + jnp.sum(ohf[:, cols], axis=1, keepdims=True)
        rank = jnp.concatenate(ranks, axis=1)
        rowi = lax.broadcasted_iota(jnp.int32, (2 * SUBLANES, 1), 0)
        start = jnp.zeros((2 * SUBLANES, 1), F32)
        run = jnp.zeros((1, 1), F32)
        for g in range(N_GROUPS):
            start = jnp.where(rowi == g, run, start)
            seg_s[g] = run[0, 0].astype(jnp.int32)
            run = run + count[g:g + 1, :]
        seg_s[N_GROUPS] = run[0, 0].astype(jnp.int32)
        pos = jnp.sum(jnp.where(oh, start + rank, 0.0), axis=0, keepdims=True).astype(jnp.int32)
        dm = jnp.where(lax.broadcasted_iota(jnp.int32, (T, T), 0) == pos, 1.0, 0.0).astype(BF16)
        dm_s[...] = dm
        c_hi, c_lo = _split_bf16(cb_ref[...])
        moved = _dot(dm, jnp.concatenate([h_ref[...], c_hi, c_lo], axis=1))
        xs_s[...] = moved[:, 0:D].astype(BF16)
        cw_s[...] = moved[:, D:D + LANES] + moved[:, D + LANES:D + 2 * LANES]
        ys_s[...] = jnp.zeros_like(ys_s)

    g = (e * MOE_EXPERTS_PER_STEP) // EXPERTS_PER_GROUP
    lo = seg_s[g]
    hi = seg_s[g + 1]
    lane = lax.broadcasted_iota(jnp.int32, (R, LANES), 1)

    def chunk(c, carry):
        rows = pl.ds(pl.multiple_of(c * R, R), R)
        xc = xs_s[rows, :]
        cws = cw_s[rows, :]
        y = None
        for s in range(MOE_EXPERTS_PER_STEP):
            gg = _dot(xc, wg_ref[s])
            uu = _dot(xc, wu_ref[s])
            cw = jnp.sum(jnp.where(lane == e * MOE_EXPERTS_PER_STEP + s, cws, 0.0), axis=-1, keepdims=True)
            act = (gg * jax.nn.sigmoid(gg)) * uu * cw
            part = _dot(act.astype(BF16), wd_ref[s])
            y = part if y is None else y + part
        ys_s[rows, :] += y
        return carry

    lax.fori_loop(lo // R, (hi + R - 1) // R, chunk, 0)

    @pl.when(e == pl.num_programs(1) - 1)
    def _():
        y = lax.dot_general(dm_s[...], ys_s[...].astype(BF16), (((0,), (0,)), ((), ())),
                            preferred_element_type=F32)
        o_ref[...] = x_ref[...] + mod_ref[0][5:6] * y


def _moe_call(h2, comb, grp, x, mod, wg, wu, wd, S, tm):
    N, D = x.shape
    E, _, F = wg.shape
    tpb = S // tm
    return pl.pallas_call(
        _moe_kernel,
        out_shape=jax.ShapeDtypeStruct((N, D), F32),
        grid=(N // tm, E // MOE_EXPERTS_PER_STEP),
        in_specs=[pl.BlockSpec((tm, D), lambda i, e: (i, 0)),
                  pl.BlockSpec((tm, LANES), lambda i, e: (i, 0)),
                  pl.BlockSpec((SUBLANES, tm), lambda i, e: (0, i)),
                  pl.BlockSpec((tm, D), lambda i, e: (i, 0)),
                  pl.BlockSpec((1, SUBLANES, D), lambda i, e: (i // tpb, 0, 0)),
                  pl.BlockSpec((MOE_EXPERTS_PER_STEP, D, F), lambda i, e: (e, 0, 0)),
                  pl.BlockSpec((MOE_EXPERTS_PER_STEP, D, F), lambda i, e: (e, 0, 0)),
                  pl.BlockSpec((MOE_EXPERTS_PER_STEP, F, D), lambda i, e: (e, 0, 0))],
        out_specs=pl.BlockSpec((tm, D), lambda i, e: (i, 0)),
        scratch_shapes=[pltpu.VMEM((tm, tm), BF16), pltpu.VMEM((tm, D), BF16),
                        pltpu.VMEM((tm, LANES), F32), pltpu.VMEM((tm, D), F32),
                        pltpu.SMEM((SUBLANES,), jnp.int32)],
        compiler_params=_cparams(("arbitrary", "arbitrary")),
        name="moe",
    )(h2, comb, grp, x, mod, wg, wu, wd)


def _tiles(S):
    t = lambda want: min(want, S)
    return dict(inproj=t(512), lru=t(256), pool=t(512), tq=t(1024), tk=t(1024),
                merge=t(512), moe=t(1024))


def kernel(x, c, w_ada, b_ada, norm1_g, norm2_g, w_in, conv_w, conv_b, lru_wa, lru_ba, lru_wx, lru_bx,
           lru_l, pool_w, pool_scale, qn_g, kn_g, lam_q1, lam_k1, lam_q2, lam_k2, subln_g, w_branch,
           w_out, router_w, router_b, w_gate, w_up, w_down):
    B, S, D = x.shape
    L = w_ada.shape[0]
    N = B * S
    W = D
    ts = _tiles(S)
    mod = _mod_call(c, w_ada, b_ada)
    cs, sn = _rope_tables(S)
    rw = jnp.pad(router_w, ((0, 0), (0, LANES - N_EXPERTS)))
    rb = router_b.reshape(N_EXPERTS, 1)
    hd = ATT_HEAD_DIM
    xf = x.reshape(N, D)
    for l in range(L):
        lam_init = 0.8 - 0.6 * math.exp(-0.3 * l)
        w_l = w_in[l].astype(BF16)
        g1 = norm1_g[l].reshape(1, D)
        qg = jnp.tile(qn_g[l], 2).reshape(1, LANES)
        kg = jnp.tile(kn_g[l], 2).reshape(1, LANES)
        proj = _inproj_call(xf, mod[l], g1, w_l[:, 0:3 * D], S, ts["inproj"])
        qkv = _inproj_call(xf, mod[l], g1, w_l[:, 3 * D:6 * D], S, ts["inproj"], rope=(cs, sn, qg, kg))
        gates = _inproj_call(xf, mod[l], g1, w_l[:, 6 * D:9 * D], S, ts["inproj"])
        br_l = _lru_call(proj, conv_w[l], conv_b[l].reshape(1, W), lru_wa[l].astype(BF16),
                         lru_ba[l].reshape(1, W), lru_wx[l].astype(BF16), lru_bx[l].reshape(1, W),
                         lru_l[l].reshape(1, W), S, W, ts["lru"])
        br_p = _pool_call(proj, pool_w[l].astype(BF16), pool_scale[l].reshape(1, W), S, W, ts["pool"])
        lamp = jnp.zeros((SUBLANES, LANES), F32)
        lamp = lamp.at[0, :hd].set(lam_q1[l]).at[1, :hd].set(lam_k1[l])
        lamp = lamp.at[2, :hd].set(lam_q2[l]).at[3, :hd].set(lam_k2[l])
        bound = math.sqrt(hd) * jnp.max(jnp.abs(qn_g[l])) * jnp.max(jnp.abs(kn_g[l]))
        safe = (bound < SAFE_SCORE_BOUND).astype(jnp.int32).reshape(1)
        br_a = _attn_call(qkv, safe, lamp, subln_g[l].reshape(1, LANES), B, S, lam_init,
                          ts["tq"], ts["tk"])
        xf, h2, comb, grp = _merge_call(xf, br_l, br_p, br_a, gates, mod[l], w_branch[l].astype(BF16),
                                        w_out[l].astype(BF16), norm2_g[l].reshape(1, D), rw, rb, S,
                                        ts["merge"])
        xf = _moe_call(h2, comb, grp, xf, mod[l], w_gate[l].astype(BF16), w_up[l].astype(BF16),
                       w_down[l].astype(BF16), S, ts["moe"])
    return xf.reshape(B, S, D)
```

```python
import functools
import math

import jax
import jax.numpy as jnp
from jax import lax
from jax.experimental import pallas as pl
from jax.experimental.pallas import tpu as pltpu

F32 = jnp.float32
BF16 = jnp.bfloat16

EPS = 1e-6
LANES = 128
SUBLANES = 8
LRU_BLOCKS = 8
CONV_WIDTH = 4
LRU_C = 8.0
POOL_WINDOWS = (2, 4, 8, 16)
POOL_HALO = 16
ATT_HEADS = 8
ATT_HEAD_DIM = 64
QK_BLOCK_COLS = 512
ATT_KV_UNROLLS = (4, 2)
ATT_VT_ROWS = 144
ROPE_THETA = 10000.0
N_BRANCH = 3
N_EXPERTS = 16
N_GROUPS = 4
EXPERTS_PER_GROUP = N_EXPERTS // N_GROUPS
MERGE_ROW_SPLITS = 2
MOE_CHUNK_ROWS = 128
MOE_EXPERTS_PER_STEP = 2
MOE_RANK_BLOCK = 256
VMEM_LIMIT = 56 * 1024 * 1024
LOG2E = math.log2(math.e)
SAFE_SCORE_BOUND = 32.0


def _cparams(sem):
    return pltpu.CompilerParams(dimension_semantics=sem, vmem_limit_bytes=VMEM_LIMIT)


def _dot(a, b):
    return jnp.dot(a, b, preferred_element_type=F32)


def _dot_nt(a, b):
    return lax.dot_general(a, b, (((1,), (1,)), ((), ())), preferred_element_type=F32)


def _mod_kernel(c_ref, w_ref, b_ref, o_ref):
    c = c_ref[...]
    ca = c * jax.nn.sigmoid(c)
    o_ref[0] = jnp.dot(ca, w_ref[0], preferred_element_type=F32,
                       precision=lax.Precision.HIGHEST) + b_ref[0]


def _mod_call(c, w_ada, b_ada):
    L, D, D6 = w_ada.shape
    B = c.shape[0]
    cp = jnp.zeros((SUBLANES, D), F32).at[:B].set(c)
    out = pl.pallas_call(
        _mod_kernel,
        out_shape=jax.ShapeDtypeStruct((L, SUBLANES, D6), F32),
        grid=(L, D6 // D),
        in_specs=[pl.BlockSpec((SUBLANES, D), lambda l, j: (0, 0)),
                  pl.BlockSpec((1, D, D), lambda l, j: (l, 0, j)),
                  pl.BlockSpec((1, 1, D), lambda l, j: (l, 0, j))],
        out_specs=pl.BlockSpec((1, SUBLANES, D), lambda l, j: (l, 0, j)),
        compiler_params=_cparams(("arbitrary", "arbitrary")),
        name="mod",
    )(cp, w_ada, b_ada.reshape(L, 1, D6))
    mod = out[:, :B].reshape(L, B, 6, D)
    return jnp.pad(mod, ((0, 0), (0, 0), (0, 2), (0, 0)))


def _qk_norm_rope(x, ssq, g, cs, sn, lower, scale):
    hd = ATT_HEAD_DIM
    xn = x * lax.rsqrt(ssq * (1.0 / hd) + EPS) * g
    partner = jnp.where(lower, pltpu.roll(xn, LANES - hd // 2, axis=1), pltpu.roll(xn, hd // 2, axis=1))
    return (xn * cs + partner * sn) * scale


def _inproj_kernel(x_ref, mod_ref, g_ref, w_ref, *rest, qk_cols):
    o_ref = rest[-1]
    x = x_ref[...]
    m = mod_ref[0]
    var = jnp.mean(x * x, axis=-1, keepdims=True)
    y = x * lax.rsqrt(var + EPS) * g_ref[...]
    h = y * (1.0 + m[1:2]) + m[0:1]
    hb = h.astype(BF16)
    if not qk_cols:
        o_ref[...] = _dot(hb, w_ref[...]).astype(BF16)
        return
    cs_ref, sn_ref, qg_ref, kg_ref = rest[:4]
    T = x.shape[0]
    lane = lax.broadcasted_iota(jnp.int32, (T, LANES), 1)
    lower = (lane % ATT_HEAD_DIM) < (ATT_HEAD_DIM // 2)
    cs, sn = cs_ref[...], sn_ref[...]
    half = qk_cols // 2
    bw = QK_BLOCK_COLS
    comp = (lax.broadcasted_iota(jnp.int32, (bw, bw), 0) // ATT_HEAD_DIM
            == lax.broadcasted_iota(jnp.int32, (bw, bw), 1) // ATT_HEAD_DIM)
    comp_ones = jnp.where(comp, 1.0, 0.0).astype(BF16)
    for c0 in range(0, o_ref.shape[1], bw):
        res = _dot(hb, w_ref[:, c0:c0 + bw])
        if c0 >= qk_cols:
            o_ref[:, c0:c0 + bw] = res.astype(BF16)
            continue
        is_q = c0 < half
        g = qg_ref[...] if is_q else kg_ref[...]
        scale = ATT_HEAD_DIM ** -0.5 * LOG2E if is_q else 1.0
        ssq = _dot((res * res).astype(BF16), comp_ones)
        for c in range(0, bw, LANES):
            blk = _qk_norm_rope(res[:, c:c + LANES], ssq[:, c:c + LANES], g, cs, sn, lower, scale)
            o_ref[:, c0 + c:c0 + c + LANES] = blk.astype(BF16)


def _inproj_call(x, mod, g, w, S, tm, rope=None):
    N, D = x.shape
    W = w.shape[1]
    tpb = S // tm
    in_specs = [pl.BlockSpec((tm, D), lambda i: (i, 0)),
                pl.BlockSpec((1, SUBLANES, D), lambda i: (i // tpb, 0, 0)),
                pl.BlockSpec((1, D), lambda i: (0, 0)),
                pl.BlockSpec((D, W), lambda i: (0, 0))]
    args = [x, mod, g, w]
    if rope is not None:
        in_specs += [pl.BlockSpec((tm, LANES), lambda i: (i % tpb, 0)),
                     pl.BlockSpec((tm, LANES), lambda i: (i % tpb, 0)),
                     pl.BlockSpec((1, LANES), lambda i: (0, 0)),
                     pl.BlockSpec((1, LANES), lambda i: (0, 0))]
        args += list(rope)
    return pl.pallas_call(
        functools.partial(_inproj_kernel, qk_cols=0 if rope is None else 2 * W // 3),
        out_shape=jax.ShapeDtypeStruct((N, W), BF16),
        grid=(N // tm,),
        in_specs=in_specs,
        out_specs=pl.BlockSpec((tm, W), lambda i: (i, 0)),
        compiler_params=_cparams(("arbitrary",)),
        name="inproj_qkv" if rope is not None else "inproj",
    )(*args)


def _lru_kernel(x_ref, y_ref, cw_ref, cb_ref, wa_ref, ba_ref, wx_ref, bx_ref, ll_ref, o_ref,
                xbuf, hcar, *, tpb):
    T, W = x_ref.shape
    blk = W // LRU_BLOCKS
    i = pl.program_id(0)

    @pl.when(i % tpb == 0)
    def _():
        xbuf[0:SUBLANES, :] = jnp.zeros((SUBLANES, W), F32)
        hcar[...] = jnp.zeros_like(hcar)

    xbuf[SUBLANES:SUBLANES + T, :] = x_ref[...].astype(F32)
    xc = cb_ref[...] + xbuf[SUBLANES:SUBLANES + T, :] * cw_ref[CONV_WIDTH - 1:CONV_WIDTH, :]
    for j in range(CONV_WIDTH - 1):
        d = CONV_WIDTH - 1 - j
        xc = xc + xbuf[SUBLANES - d:SUBLANES - d + T, :] * cw_ref[j:j + 1, :]
    xbuf[0:SUBLANES, :] = xbuf[T:T + SUBLANES, :]

    xcb = xc.astype(BF16)
    ga, gx = [], []
    for n in range(LRU_BLOCKS):
        xs = xcb[:, n * blk:(n + 1) * blk]
        ga.append(_dot(xs, wa_ref[n]))
        gx.append(_dot(xs, wx_ref[n]))
    gate_a = jax.nn.sigmoid(jnp.concatenate(ga, axis=1) + ba_ref[...])
    gate_x = jax.nn.sigmoid(jnp.concatenate(gx, axis=1) + bx_ref[...])
    ll = ll_ref[...]
    log_sig = jnp.minimum(ll, 0.0) - jnp.log1p(jnp.exp(-jnp.abs(ll)))
    log_a = LRU_C * gate_a * log_sig
    a = jnp.exp(log_a)
    om = 1.0 - a * a
    u = om * lax.rsqrt(jnp.maximum(om, 1e-30)) * (gate_x * xc)

    G = T // SUBLANES
    A = a.reshape(G, SUBLANES, W)
    H = u.reshape(G, SUBLANES, W)
    sub = lax.broadcasted_iota(jnp.int32, (G, SUBLANES, W), 1)
    sh = 1
    while sh < SUBLANES:
        keep = sub >= sh
        A_s = jnp.where(keep, pltpu.roll(A, sh, axis=1), 1.0)
        H_s = jnp.where(keep, pltpu.roll(H, sh, axis=1), 0.0)
        H = A * H_s + H
        A = A * A_s
        sh *= 2
    gy = jax.nn.gelu(y_ref[...].astype(F32))
    carry = hcar[...]
    hs = []
    for g in range(G):
        hs.append(H[g] + A[g] * carry)
        carry = H[g, SUBLANES - 1:SUBLANES, :] + A[g, SUBLANES - 1:SUBLANES, :] * carry
        if g % 2 == 1:
            rows = slice((g - 1) * SUBLANES, (g + 1) * SUBLANES)
            h = jnp.concatenate(hs, axis=0)
            o_ref[rows, :] = (h * gy[rows, :]).astype(o_ref.dtype)
            hs = []
    hcar[...] = carry


def _lru_call(proj, cw, cb, wa, ba, wx, bx, ll, S, W, tm):
    N = proj.shape[0]
    tpb = S // tm
    blk = W // LRU_BLOCKS
    vec = lambda: pl.BlockSpec((1, W), lambda i: (0, 0))
    return pl.pallas_call(
        functools.partial(_lru_kernel, tpb=tpb),
        out_shape=jax.ShapeDtypeStruct((N, W), BF16),
        grid=(N // tm,),
        in_specs=[pl.BlockSpec((tm, W), lambda i: (i, 0)),
                  pl.BlockSpec((tm, W), lambda i: (i, 1)),
                  pl.BlockSpec((CONV_WIDTH, W), lambda i: (0, 0)),
                  vec(),
                  pl.BlockSpec((LRU_BLOCKS, blk, blk), lambda i: (0, 0, 0)),
                  vec(),
                  pl.BlockSpec((LRU_BLOCKS, blk, blk), lambda i: (0, 0, 0)),
                  vec(),
                  vec()],
        out_specs=pl.BlockSpec((tm, W), lambda i: (i, 0)),
        scratch_shapes=[pltpu.VMEM((tm + SUBLANES, W), F32), pltpu.VMEM((1, W), F32)],
        compiler_params=_cparams(("arbitrary",)),
        name="lru",
    )(proj, proj, cw, cb, wa, ba, wx, bx, ll)


def _pool_kernel(x_ref, pw_ref, ps_ref, o_ref, xbuf, *, tpb):
    T, W = x_ref.shape
    G = len(POOL_WINDOWS)
    gw = W // G
    i = pl.program_id(0)

    @pl.when(i % tpb == 0)
    def _():
        xbuf[0:POOL_HALO, :] = jnp.zeros((POOL_HALO, W), F32)

    xbuf[POOL_HALO:POOL_HALO + T, :] = x_ref[...].astype(F32)
    t = (i % tpb) * T + lax.broadcasted_iota(jnp.int32, (T, 1), 0)
    outs = []
    for g, w in enumerate(POOL_WINDOWS):
        cols = slice(g * gw, (g + 1) * gw)
        xg = xbuf[POOL_HALO:POOL_HALO + T, cols]
        ws = xg
        for d in range(1, w):
            ws = ws + xbuf[POOL_HALO - d:POOL_HALO - d + T, cols]
        cnt = jnp.minimum(t + 1, w).astype(F32)
        pooled = (ws / cnt - xg).astype(BF16)
        outs.append(_dot(pooled, pw_ref[g]))
    xbuf[0:POOL_HALO, :] = xbuf[T:T + POOL_HALO, :]
    o_ref[...] = (jnp.concatenate(outs, axis=1) * ps_ref[...]).astype(o_ref.dtype)


def _pool_call(proj, pw, ps, S, W, tm):
    N = proj.shape[0]
    tpb = S // tm
    G = len(POOL_WINDOWS)
    return pl.pallas_call(
        functools.partial(_pool_kernel, tpb=tpb),
        out_shape=jax.ShapeDtypeStruct((N, W), BF16),
        grid=(N // tm,),
        in_specs=[pl.BlockSpec((tm, W), lambda i: (i, 2)),
                  pl.BlockSpec((G, W // G, W // G), lambda i: (0, 0, 0)),
                  pl.BlockSpec((1, W), lambda i: (0, 0))],
        out_specs=pl.BlockSpec((tm, W), lambda i: (i, 0)),
        scratch_shapes=[pltpu.VMEM((tm + POOL_HALO, W), F32)],
        compiler_params=_cparams(("arbitrary",)),
        name="pool",
    )(proj, pw, ps)


def _rope_tables(S):
    hd = ATT_HEAD_DIM
    pos = jnp.arange(S, dtype=F32)
    inv = ROPE_THETA ** (-jnp.arange(0, hd, 2, dtype=F32) / hd)
    ang = pos[:, None] * inv[None, :]
    c, s = jnp.cos(ang), jnp.sin(ang)
    cs = jnp.concatenate([c, c, c, c], axis=1)
    sn = jnp.concatenate([-s, s, -s, s], axis=1)
    return cs, sn


def _attn_kernel(safe_ref, q_ref, k_ref, v_ref, lp_ref, sg_ref, o_ref, vt_s, acct_s, m_s, l_s, acc_s, *,
                 tk, lam_init):
    tq = q_ref.shape[0]
    S = k_ref.shape[0]
    i = pl.program_id(2)
    lane = lax.broadcasted_iota(jnp.int32, (tq, LANES), 1)
    q = q_ref[...]
    zero = jnp.zeros_like(q)
    qc = (jnp.where(lane < ATT_HEAD_DIM, q, zero), jnp.where(lane < ATT_HEAD_DIM, zero, q))
    safe = safe_ref[0] == 1
    n_full = (i * tq) // tk
    n_all = ((i + 1) * tq + tk - 1) // tk

    def scores(c, j, k, masked):
        s = _dot_nt(qc[c], k)
        if masked:
            qpos = i * tq + lax.broadcasted_iota(jnp.int32, (tq, tk), 0)
            kpos = j * tk + lax.broadcasted_iota(jnp.int32, (tq, tk), 1)
            s = jnp.where(kpos <= qpos, s, -jnp.inf)
        return s

    def loops(tile, unrolls=()):
        start = 0
        for u in unrolls:
            n_groups = (n_full - start) // u

            def group(jj, c, u=u, start=start):
                for t in range(u):
                    tile(start + u * jj + t, False)
                return c

            lax.fori_loop(0, n_groups, group, 0)
            start = start + u * n_groups
        lax.fori_loop(start, n_full, lambda j, c: tile(j, False) or c, 0)
        lax.fori_loop(n_full, n_all, lambda j, c: tile(j, True) or c, 0)

    def finalize(a1, l1, a2, l2):
        lp = lp_ref[...]
        lam = (jnp.exp(jnp.sum(lp[0:1] * lp[1:2], axis=-1, keepdims=True))
               - jnp.exp(jnp.sum(lp[2:3] * lp[3:4], axis=-1, keepdims=True)) + lam_init)
        o = a1 / l1 - lam * (a2 / l2)
        o = o * lax.rsqrt(jnp.mean(o * o, axis=-1, keepdims=True) + EPS) * sg_ref[...]
        o_ref[...] = (o * (1.0 - lam_init)).astype(o_ref.dtype)

    @pl.when(jnp.logical_and(safe, i == 0))
    def _():
        tail = jnp.where(lax.broadcasted_iota(jnp.int32, (ATT_VT_ROWS - LANES, tk), 0) == 0,
                         1.0, 0.0).astype(BF16)

        def fill(j, c):
            off = pl.multiple_of(j * tk, tk)
            vt_s[0:LANES, pl.ds(off, tk)] = v_ref[pl.ds(off, tk), :].astype(F32).T.astype(BF16)
            vt_s[LANES:ATT_VT_ROWS, pl.ds(off, tk)] = tail
            return c

        lax.fori_loop(0, S // tk, fill, 0)

    @pl.when(safe)
    def _():
        acct_s[...] = jnp.zeros_like(acct_s)

        def tile(j, masked):
            off = pl.multiple_of(j * tk, tk)
            k = k_ref[pl.ds(off, tk), :]
            vt = vt_s[:, pl.ds(off, tk)]
            if masked and tq == tk:
                hq = tq // 2
                tri = (lax.broadcasted_iota(jnp.int32, (hq, hq), 0)
                       <= lax.broadcasted_iota(jnp.int32, (hq, hq), 1))
                for c in range(2):
                    s_a = jnp.where(tri, _dot_nt(k[0:hq], qc[c][0:hq]), -jnp.inf)
                    acct_s[c, :, 0:hq] += _dot(vt[:, 0:hq], jnp.exp2(s_a).astype(BF16))
                    s_b = _dot_nt(k, qc[c][hq:tq])
                    s_b = jnp.concatenate([s_b[0:hq], jnp.where(tri, s_b[hq:tk], -jnp.inf)], axis=0)
                    acct_s[c, :, hq:tq] += _dot(vt, jnp.exp2(s_b).astype(BF16))
                return
            for c in range(2):
                s = _dot_nt(k, qc[c])
                if masked:
                    kpos = j * tk + lax.broadcasted_iota(jnp.int32, (tk, tq), 0)
                    qpos = i * tq + lax.broadcasted_iota(jnp.int32, (tk, tq), 1)
                    s = jnp.where(kpos <= qpos, s, -jnp.inf)
                acct_s[c] += _dot(vt, jnp.exp2(s).astype(BF16))

        loops(tile, unrolls=ATT_KV_UNROLLS)
        a1, a2 = acct_s[0], acct_s[1]
        lp = lp_ref[...]
        lam = (jnp.exp(jnp.sum(lp[0:1] * lp[1:2], axis=-1, keepdims=True))
               - jnp.exp(jnp.sum(lp[2:3] * lp[3:4], axis=-1, keepdims=True)) + lam_init)
        o = a1[0:LANES] / a1[LANES:LANES + 1] - lam * (a2[0:LANES] / a2[LANES:LANES + 1])
        o = o * lax.rsqrt(jnp.mean(o * o, axis=0, keepdims=True) + EPS)
        o_ref[...] = (o.T * sg_ref[...] * (1.0 - lam_init)).astype(o_ref.dtype)

    @pl.when(jnp.logical_not(safe))
    def _():
        m_s[...] = jnp.full_like(m_s, -jnp.inf)
        l_s[...] = jnp.zeros_like(l_s)
        acc_s[...] = jnp.zeros_like(acc_s)

        def tile(j, masked):
            off = pl.multiple_of(j * tk, tk)
            k = k_ref[pl.ds(off, tk), :]
            v = v_ref[pl.ds(off, tk), :]
            for c in range(2):
                s = scores(c, j, k, masked)
                m_old = m_s[c]
                m_new = jnp.maximum(m_old, jnp.max(s, axis=-1, keepdims=True))
                alpha = jnp.exp2(m_old - m_new)
                p = jnp.exp2(s - m_new)
                l_s[c] = alpha * l_s[c] + jnp.sum(p, axis=-1, keepdims=True)
                acc_s[c] = alpha * acc_s[c] + _dot(p.astype(BF16), v)
                m_s[c] = m_new

        loops(tile)
        finalize(acc_s[0], l_s[0], acc_s[1], l_s[1])


def _attn_call(qkv, safe, lamp, sg, B, S, lam_init, tq, tk):
    N = qkv.shape[0]
    W = qkv.shape[1] // 3
    H = W // LANES
    nq = S // tq
    grid_spec = pltpu.PrefetchScalarGridSpec(
        num_scalar_prefetch=1,
        grid=(B, H, nq),
        in_specs=[pl.BlockSpec((tq, LANES), lambda b, h, i, f: (b * nq + i, h)),
                  pl.BlockSpec((S, LANES), lambda b, h, i, f: (b, H + h)),
                  pl.BlockSpec((S, LANES), lambda b, h, i, f: (b, 2 * H + h)),
                  pl.BlockSpec((SUBLANES, LANES), lambda b, h, i, f: (0, 0)),
                  pl.BlockSpec((1, LANES), lambda b, h, i, f: (0, 0))],
        out_specs=pl.BlockSpec((tq, LANES), lambda b, h, i, f: (b * nq + i, h)),
        scratch_shapes=[pltpu.VMEM((ATT_VT_ROWS, S), BF16), pltpu.VMEM((2, ATT_VT_ROWS, tq), F32),
                        pltpu.VMEM((2, tq, 1), F32), pltpu.VMEM((2, tq, 1), F32),
                        pltpu.VMEM((2, tq, LANES), F32)])
    return pl.pallas_call(
        functools.partial(_attn_kernel, tk=tk, lam_init=lam_init),
        out_shape=jax.ShapeDtypeStruct((N, W), BF16),
        grid_spec=grid_spec,
        compiler_params=_cparams(("arbitrary", "arbitrary", "arbitrary")),
        name="attn",
    )(safe, qkv, qkv, qkv, lamp, sg)


def _split_bf16(x):
    hi = x.astype(BF16)
    lo = (x - hi.astype(F32)).astype(BF16)
    return hi, lo


def _route(sel, aff):
    epg = EXPERTS_PER_GROUP
    T = sel.shape[1]
    rows = [sel[e:e + 1, :] for e in range(N_EXPERTS)]
    arow = [aff[e:e + 1, :] for e in range(N_EXPERTS)]
    best = None
    gidx = None
    for g in range(N_GROUPS):
        r = rows[g * epg:(g + 1) * epg]
        gs = None
        for a in range(epg):
            for b in range(a + 1, epg):
                pair = r[a] + r[b]
                gs = pair if gs is None else jnp.maximum(gs, pair)
        if best is None:
            best, gidx = gs, jnp.zeros((1, T), jnp.int32)
        else:
            better = gs > best
            gidx = jnp.where(better, g, gidx)
            best = jnp.where(better, gs, best)
    ig, ag = [], []
    for a in range(epg):
        s_a, f_a = rows[a], arow[a]
        for g in range(1, N_GROUPS):
            pick = gidx == g
            s_a = jnp.where(pick, rows[g * epg + a], s_a)
            f_a = jnp.where(pick, arow[g * epg + a], f_a)
        ig.append(s_a)
        ag.append(f_a)
    v1, a1, w1 = ig[0], jnp.zeros((1, T), jnp.int32), ag[0]
    for a in range(1, epg):
        better = ig[a] > v1
        a1 = jnp.where(better, a, a1)
        w1 = jnp.where(better, ag[a], w1)
        v1 = jnp.where(better, ig[a], v1)
    v2 = jnp.full((1, T), -jnp.inf, F32)
    a2 = jnp.zeros((1, T), jnp.int32)
    w2 = jnp.zeros((1, T), F32)
    for a in range(epg):
        better = jnp.logical_and(a1 != a, ig[a] > v2)
        a2 = jnp.where(better, a, a2)
        w2 = jnp.where(better, ag[a], w2)
        v2 = jnp.where(better, ig[a], v2)
    tot = w1 + w2
    e1 = gidx * epg + a1
    e2 = gidx * epg + a2
    erow = lax.broadcasted_iota(jnp.int32, (N_EXPERTS, T), 0)
    comb = jnp.where(erow == e1, w1 / tot, 0.0) + jnp.where(erow == e2, w2 / tot, 0.0)
    return comb, gidx


def _merge_kernel(x_ref, bl_ref, bp_ref, ba_ref, gt_ref, mod_ref, wb_ref, wo_ref, g2_ref, rw_ref,
                  rb_ref, xo_ref, h2_ref, cb_ref, gr_ref):
    T, D = x_ref.shape
    m = mod_ref[0]
    w_hi, w_lo = _split_bf16(rw_ref[...])
    for half in range(MERGE_ROW_SPLITS):
        Th = T // MERGE_ROW_SPLITS
        rows = slice(half * Th, (half + 1) * Th)
        merged = None
        for n, br in enumerate((bl_ref, bp_ref, ba_ref)):
            gate = jax.nn.sigmoid(gt_ref[rows, n * D:(n + 1) * D].astype(F32))
            term = gate * _dot(br[rows, :], wb_ref[n])
            merged = term if merged is None else merged + term
        y = _dot(merged.astype(BF16), wo_ref[...])
        x = x_ref[rows, :] + m[2:3] * y
        xo_ref[rows, :] = x
        var = jnp.mean(x * x, axis=-1, keepdims=True)
        h2 = x * lax.rsqrt(var + EPS) * g2_ref[...] * (1.0 + m[4:5]) + m[3:4]
        h2_ref[rows, :] = h2.astype(BF16)
        h_hi, h_lo = _split_bf16(h2)
        logits = _dot(h_hi, w_hi) + _dot(h_hi, w_lo) + _dot(h_lo, w_hi)
        lt = logits.T[0:N_EXPERTS, :]
        aff = jax.nn.sigmoid(lt)
        comb, gidx = _route(aff + rb_ref[...], aff)
        comb = jnp.concatenate([comb, jnp.zeros((LANES - N_EXPERTS, Th), F32)], axis=0)
        cb_ref[rows, :] = comb.T
        gr_ref[:, rows] = jnp.broadcast_to(gidx.astype(F32), (SUBLANES, Th))


def _merge_call(x, br_l, br_p, br_a, gates, mod, wb, wo, g2, rw, rb, S, tm):
    N, D = x.shape
    tpb = S // tm
    tok = lambda: pl.BlockSpec((tm, D), lambda i: (i, 0))
    return pl.pallas_call(
        _merge_kernel,
        out_shape=(jax.ShapeDtypeStruct((N, D), F32), jax.ShapeDtypeStruct((N, D), BF16),
                   jax.ShapeDtypeStruct((N, LANES), F32), jax.ShapeDtypeStruct((SUBLANES, N), F32)),
        grid=(N // tm,),
        in_specs=[tok(), tok(), tok(), tok(),
                  pl.BlockSpec((tm, N_BRANCH * D), lambda i: (i, 0)),
                  pl.BlockSpec((1, SUBLANES, D), lambda i: (i // tpb, 0, 0)),
                  pl.BlockSpec((N_BRANCH, D, D), lambda i: (0, 0, 0)),
                  pl.BlockSpec((D, D), lambda i: (0, 0)),
                  pl.BlockSpec((1, D), lambda i: (0, 0)),
                  pl.BlockSpec((D, LANES), lambda i: (0, 0)),
                  pl.BlockSpec((N_EXPERTS, 1), lambda i: (0, 0))],
        out_specs=(tok(), tok(), pl.BlockSpec((tm, LANES), lambda i: (i, 0)),
                   pl.BlockSpec((SUBLANES, tm), lambda i: (0, i))),
        compiler_params=_cparams(("arbitrary",)),
        name="merge",
    )(x, br_l, br_p, br_a, gates, mod, wb, wo, g2, rw, rb)


def _moe_kernel(h_ref, cb_ref, gr_ref, x_ref, mod_ref, wg_ref, wu_ref, wd_ref, o_ref,
                dm_s, xs_s, cw_s, ys_s, seg_s):
    T, D = h_ref.shape
    R = MOE_CHUNK_ROWS
    e = pl.program_id(1)

    @pl.when(e == 0)
    def _():
        gid = gr_ref[0:1, :]
        grow = lax.broadcasted_iota(jnp.int32, (2 * SUBLANES, T), 0).astype(F32)
        oh = grow == gid
        ohf = jnp.where(oh, 1.0, 0.0)
        ohb = ohf.astype(BF16)
        rb = MOE_RANK_BLOCK
        tri = jnp.where(lax.broadcasted_iota(jnp.int32, (rb, rb), 0)
                        < lax.broadcasted_iota(jnp.int32, (rb, rb), 1), 1.0, 0.0).astype(BF16)
        count = jnp.zeros((2 * SUBLANES, 1), F32)
        ranks = []
        for b in range(T // rb):
            cols = slice(b * rb, (b + 1) * rb)
            ranks.append(_dot(ohb[:, cols], tri) + count)
            count = count + jnp.sum(ohf[:, cols], axis=1, keepdims=True)
        rank = jnp.concatenate(ranks, axis=1)
        rowi = lax.broadcasted_iota(jnp.int32, (2 * SUBLANES, 1), 0)
        start = jnp.zeros((2 * SUBLANES, 1), F32)
        run = jnp.zeros((1, 1), F32)
        for g in range(N_GROUPS):
            start = jnp.where(rowi == g, run, start)
            seg_s[g] = run[0, 0].astype(jnp.int32)
            run = run + count[g:g + 1, :]
        seg_s[N_GROUPS] = run[0, 0].astype(jnp.int32)
        pos = jnp.sum(jnp.where(oh, start + rank, 0.0), axis=0, keepdims=True).astype(jnp.int32)
        dm = jnp.where(lax.broadcasted_iota(jnp.int32, (T, T), 0) == pos, 1.0, 0.0).astype(BF16)
        dm_s[...] = dm
        c_hi, c_lo = _split_bf16(cb_ref[...])
        moved = _dot(dm, jnp.concatenate([h_ref[...], c_hi, c_lo], axis=1))
        xs_s[...] = moved[:, 0:D].astype(BF16)
        cw_s[...] = moved[:, D:D + LANES] + moved[:, D + LANES:D + 2 * LANES]
        ys_s[...] = jnp.zeros_like(ys_s)

    g = (e * MOE_EXPERTS_PER_STEP) // EXPERTS_PER_GROUP
    lo = seg_s[g]
    hi = seg_s[g + 1]
    lane = lax.broadcasted_iota(jnp.int32, (R, LANES), 1)

    def chunk(c, carry):
        rows = pl.ds(pl.multiple_of(c * R, R), R)
        xc = xs_s[rows, :]
        cws = cw_s[rows, :]
        y = None
        for s in range(MOE_EXPERTS_PER_STEP):
            gg = _dot(xc, wg_ref[s])
            uu = _dot(xc, wu_ref[s])
            cw = jnp.sum(jnp.where(lane == e * MOE_EXPERTS_PER_STEP + s, cws, 0.0), axis=-1, keepdims=True)
            act = (gg * jax.nn.sigmoid(gg)) * uu * cw
            part = _dot(act.astype(BF16), wd_ref[s])
            y = part if y is None else y + part
        ys_s[rows, :] += y
        return carry

    lax.fori_loop(lo // R, (hi + R - 1) // R, chunk, 0)

    @pl.when(e == pl.num_programs(1) - 1)
    def _():
        y = lax.dot_general(dm_s[...], ys_s[...].astype(BF16), (((0,), (0,)), ((), ())),
                            preferred_element_type=F32)
        o_ref[...] = x_ref[...] + mod_ref[0][5:6] * y


def _moe_call(h2, comb, grp, x, mod, wg, wu, wd, S, tm):
    N, D = x.shape
    E, _, F = wg.shape
    tpb = S // tm
    return pl.pallas_call(
        _moe_kernel,
        out_shape=jax.ShapeDtypeStruct((N, D), F32),
        grid=(N // tm, E // MOE_EXPERTS_PER_STEP),
        in_specs=[pl.BlockSpec((tm, D), lambda i, e: (i, 0)),
                  pl.BlockSpec((tm, LANES), lambda i, e: (i, 0)),
                  pl.BlockSpec((SUBLANES, tm), lambda i, e: (0, i)),
                  pl.BlockSpec((tm, D), lambda i, e: (i, 0)),
                  pl.BlockSpec((1, SUBLANES, D), lambda i, e: (i // tpb, 0, 0)),
                  pl.BlockSpec((MOE_EXPERTS_PER_STEP, D, F), lambda i, e: (e, 0, 0)),
                  pl.BlockSpec((MOE_EXPERTS_PER_STEP, D, F), lambda i, e: (e, 0, 0)),
                  pl.BlockSpec((MOE_EXPERTS_PER_STEP, F, D), lambda i, e: (e, 0, 0))],
        out_specs=pl.BlockSpec((tm, D), lambda i, e: (i, 0)),
        scratch_shapes=[pltpu.VMEM((tm, tm), BF16), pltpu.VMEM((tm, D), BF16),
                        pltpu.VMEM((tm, LANES), F32), pltpu.VMEM((tm, D), F32),
                        pltpu.SMEM((SUBLANES,), jnp.int32)],
        compiler_params=_cparams(("arbitrary", "arbitrary")),
        name="moe",
    )(h2, comb, grp, x, mod, wg, wu, wd)


def _tiles(S):
    t = lambda want: min(want, S)
    return dict(inproj=t(512), lru=t(512), pool=t(512), tq=t(1024), tk=t(1024),
                merge=t(512), moe=t(1024))


def kernel(x, c, w_ada, b_ada, norm1_g, norm2_g, w_in, conv_w, conv_b, lru_wa, lru_ba, lru_wx, lru_bx,
           lru_l, pool_w, pool_scale, qn_g, kn_g, lam_q1, lam_k1, lam_q2, lam_k2, subln_g, w_branch,
           w_out, router_w, router_b, w_gate, w_up, w_down):
    B, S, D = x.shape
    L = w_ada.shape[0]
    N = B * S
    W = D
    ts = _tiles(S)
    mod = _mod_call(c, w_ada, b_ada)
    cs, sn = _rope_tables(S)
    rw = jnp.pad(router_w, ((0, 0), (0, LANES - N_EXPERTS)))
    rb = router_b.reshape(N_EXPERTS, 1)
    hd = ATT_HEAD_DIM
    xf = x.reshape(N, D)
    for l in range(L):
        lam_init = 0.8 - 0.6 * math.exp(-0.3 * l)
        w_l = w_in[l].astype(BF16)
        g1 = norm1_g[l].reshape(1, D)
        qg = jnp.tile(qn_g[l], 2).reshape(1, LANES)
        kg = jnp.tile(kn_g[l], 2).reshape(1, LANES)
        proj = _inproj_call(xf, mod[l], g1, w_l[:, 0:3 * D], S, ts["inproj"])
        qkv = _inproj_call(xf, mod[l], g1, w_l[:, 3 * D:6 * D], S, ts["inproj"], rope=(cs, sn, qg, kg))
        gates = _inproj_call(xf, mod[l], g1, w_l[:, 6 * D:9 * D], S, ts["inproj"])
        br_l = _lru_call(proj, conv_w[l], conv_b[l].reshape(1, W), lru_wa[l].astype(BF16),
                         lru_ba[l].reshape(1, W), lru_wx[l].astype(BF16), lru_bx[l].reshape(1, W),
                         lru_l[l].reshape(1, W), S, W, ts["lru"])
        br_p = _pool_call(proj, pool_w[l].astype(BF16), pool_scale[l].reshape(1, W), S, W, ts["pool"])
        lamp = jnp.zeros((SUBLANES, LANES), F32)
        lamp = lamp.at[0, :hd].set(lam_q1[l]).at[1, :hd].set(lam_k1[l])
        lamp = lamp.at[2, :hd].set(lam_q2[l]).at[3, :hd].set(lam_k2[l])
        bound = math.sqrt(hd) * jnp.max(jnp.abs(qn_g[l])) * jnp.max(jnp.abs(kn_g[l]))
        safe = (bound < SAFE_SCORE_BOUND).astype(jnp.int32).reshape(1)
        br_a = _attn_call(qkv, safe, lamp, subln_g[l].reshape(1, LANES), B, S, lam_init,
                          ts["tq"], ts["tk"])
        xf, h2, comb, grp = _merge_call(xf, br_l, br_p, br_a, gates, mod[l], w_branch[l].astype(BF16),
                                        w_out[l].astype(BF16), norm2_g[l].reshape(1, D), rw, rb, S,
                                        ts["merge"])
        xf = _moe_call(h2, comb, grp, xf, mod[l], w_gate[l].astype(BF16), w_up[l].astype(BF16),
                       w_down[l].astype(BF16), S, ts["moe"])
    return xf.reshape(B, S, D)
```
